```python
import math
import jax, jax.numpy as jnp
from jax import lax
import numpy as np

D_MODEL = 1024
BATCH = 2
SEQ = 8192
DEPTH = 2
DEC_BATCH = 32
DEC_SEQ = 4
PAST_LEN = 16384
PAGE_SIZE = 128

GDN_HEADS = 4
GDN_DK = 128
GDN_DV = 128
GDN_CONV = 4
GDN_CHUNK = 64
SB_HEADS = 8
SB_DH = 64
FOX_HEADS = 8
FOX_DH = 64
FOX_F_BIAS_INIT = 3.0
Q_BLOCK = 128
D_FF = 2816
FFN_CONV = 3
EPS = 1e-6
N_BRANCH = 3

GDN_QK = GDN_HEADS * GDN_DK
GDN_V = GDN_HEADS * GDN_DV
GDN_CONV_CH = 2 * GDN_QK + GDN_V
SB_W = SB_HEADS * SB_DH
FOX_W = FOX_HEADS * FOX_DH
OFF_GDN_QKV = 0
OFF_GDN_A = OFF_GDN_QKV + GDN_CONV_CH
OFF_GDN_B = OFF_GDN_A + GDN_HEADS
OFF_GDN_Z = OFF_GDN_B + GDN_HEADS
OFF_SB = OFF_GDN_Z + GDN_V
OFF_FOX = OFF_SB + 3 * SB_W
OFF_FOX_F = OFF_FOX + 3 * FOX_W
OFF_GATE = OFF_FOX_F + FOX_HEADS
D_IN = OFF_GATE + N_BRANCH * D_MODEL

kernel_name = "hybrid_gdn_stickbreak_fox_convffn_step"


def _rmsnorm(x, g):
    xf = x.astype(jnp.float32)
    y = xf * lax.rsqrt(jnp.mean(xf * xf, axis=-1, keepdims=True) + EPS)
    return (y * g.astype(jnp.float32)).astype(x.dtype)


def _l2norm(x):
    xf = x.astype(jnp.float32)
    return xf * lax.rsqrt(jnp.sum(xf * xf, axis=-1, keepdims=True) + EPS)


def _causal_dwconv(x, buf, w):
    width = w.shape[0]
    L = x.shape[1]
    xp = jnp.concatenate([buf.astype(x.dtype), x], axis=1)
    y = sum(xp[:, i:i + L] * w[i] for i in range(width))
    return y, xp[:, L:]


def _gather_pages(pool, page_table):
    g = pool[page_table]
    return g.reshape(g.shape[0], g.shape[1] * g.shape[2], *g.shape[3:])


def _gated_delta_rule(q, k, v, g, beta, s0):
    B, L, H, DK = q.shape
    DV = v.shape[-1]
    C = GDN_CHUNK if L % GDN_CHUNK == 0 else L
    n = L // C
    f32 = jnp.float32
    q = q.astype(f32) * (DK ** -0.5)
    k = k.astype(f32)
    v = v.astype(f32)
    g = g.astype(f32)
    beta = beta.astype(f32)

    def chunks(a):
        return jnp.moveaxis(a.reshape(B, n, C, *a.shape[2:]), 1, 0)

    incl = jnp.tril(jnp.ones((C, C), dtype=bool))
    strict = jnp.tril(jnp.ones((C, C), dtype=bool), -1)
    eye = jnp.eye(C, dtype=f32)

    def step(S, inp):
        qc, kc, vc, gc, bc = inp
        gcum = jnp.cumsum(gc, axis=1)
        gh = jnp.swapaxes(gcum, 1, 2)
        diff = gh[..., :, None] - gh[..., None, :]
        decay = jnp.where(incl, jnp.exp(jnp.where(incl, diff, 0.0)), 0.0)
        kb = kc * bc[..., None]
        vb = vc * bc[..., None]
        kkt = jnp.einsum('bihk,bjhk->bhij', kb, kc) * decay
        tmat = lax.linalg.triangular_solve(eye + jnp.where(strict, kkt, 0.0),
                                           jnp.broadcast_to(eye, kkt.shape),
                                           left_side=True, lower=True, unit_diagonal=True)
        u = jnp.einsum('bhij,bjhv->bhiv', tmat, vb)
        w = jnp.einsum('bhij,bjhk->bhik', tmat, kb * jnp.exp(gcum)[..., None])
        v_new = u - jnp.einsum('bhik,bhkv->bhiv', w, S)
        qk = jnp.einsum('bihk,bjhk->bhij', qc, kc) * decay
        o = (jnp.einsum('bihk,bhkv->bhiv', qc * jnp.exp(gcum)[..., None], S)
             + jnp.einsum('bhij,bhjv->bhiv', qk, v_new))
        g_last = gcum[:, -1]
        k_dec = kc * jnp.exp(g_last[:, None] - gcum)[..., None]
        S = S * jnp.exp(g_last)[..., None, None] + jnp.einsum('bjhk,bhjv->bhkv', k_dec, v_new)
        return S, o

    s_final, o = lax.scan(step, s0.astype(f32), (chunks(q), chunks(k), chunks(v), chunks(g), chunks(beta)))
    o = jnp.moveaxis(o, 0, 1)
    o = jnp.swapaxes(o, 2, 3).reshape(B, L, H, DV)
    return o, s_final


def _stick_breaking(q, k, v, q_offset):
    B, Lq, H, Dh = q.shape
    Lk = k.shape[1]
    blk = Q_BLOCK if Lq % Q_BLOCK == 0 else Lq
    nb = Lq // blk
    kf = k.astype(jnp.float32)
    key_pos = jnp.arange(Lk)
    qb = jnp.moveaxis(q.reshape(B, nb, blk, H, Dh), 1, 0)

    def block(args):
        q_blk, start = args
        t = q_offset + start + jnp.arange(blk)
        z = jnp.einsum('bqhd,bkhd->bhqk', q_blk.astype(jnp.float32), kf) * (Dh ** -0.5)
        earlier = key_pos[None, :] < t[:, None]
        log_fail = jnp.where(earlier, jax.nn.log_sigmoid(-z), 0.0)
        log_between = lax.cumsum(log_fail, axis=3, reverse=True) - log_fail
        a = jnp.where(earlier, jnp.exp(jax.nn.log_sigmoid(z) + log_between), 0.0)
        return jnp.einsum('bhqk,bkhd->bqhd', a.astype(v.dtype), v)

    o = lax.map(block, (qb, jnp.arange(nb) * blk))
    return jnp.moveaxis(o, 0, 1).reshape(B, Lq, H, Dh)


def _forgetting_attention(q, k, v, logf, q_offset):
    B, Lq, H, Dh = q.shape
    Lk = k.shape[1]
    blk = Q_BLOCK if Lq % Q_BLOCK == 0 else Lq
    nb = Lq // blk
    F = jnp.cumsum(logf.astype(jnp.float32), axis=1)
    Fk = jnp.swapaxes(F, 1, 2)
    Fq = F[:, q_offset:q_offset + Lq]
    kf = k.astype(jnp.float32)
    key_pos = jnp.arange(Lk)
    qb = jnp.moveaxis(q.reshape(B, nb, blk, H, Dh), 1, 0)
    fqb = jnp.moveaxis(Fq.reshape(B, nb, blk, H), 1, 0)

    def block(args):
        q_blk, fq_blk, start = args
        t = q_offset + start + jnp.arange(blk)
        logits = (jnp.einsum('bqhd,bkhd->bhqk', q_blk.astype(jnp.float32), kf) * (Dh ** -0.5)
                  + jnp.swapaxes(fq_blk, 1, 2)[..., None] - Fk[:, :, None, :])
        logits = jnp.where(key_pos[None, :] <= t[:, None], logits, -jnp.inf)
        p = jax.nn.softmax(logits, axis=-1)
        return jnp.einsum('bhqk,bkhd->bqhd', p.astype(v.dtype), v)

    o = lax.map(block, (qb, fqb, jnp.arange(nb) * blk))
    return jnp.moveaxis(o, 0, 1).reshape(B, Lq, H, Dh)


def _layer(x, c, past, s0, gdn_buf, ffn_buf,
           norm1_g, norm2_g, w_ada, b_ada, w_in, gdn_conv_w, gdn_a_log, gdn_dt_bias,
           gdn_norm_g, fox_f_bias, w_br_gdn, w_br_sb, w_br_fox, w_out,
           w_up, ffn_conv_w, ffn_conv_b, w_down):
    B, L, _ = x.shape
    f32 = jnp.float32
    mod = jnp.einsum('bd,de->be', jax.nn.silu(c), w_ada) + b_ada
    sh1, sc1, gt1, sh2, sc2, gt2 = jnp.split(mod[:, None, :], 6, axis=-1)

    u = _rmsnorm(x, norm1_g) * (1.0 + sc1) + sh1
    proj = jnp.einsum('bld,de->ble', u, w_in)

    qkv, gdn_buf_new = _causal_dwconv(proj[..., OFF_GDN_QKV:OFF_GDN_A], gdn_buf, gdn_conv_w)
    qkv = jax.nn.silu(qkv)
    gq = _l2norm(qkv[..., :GDN_QK].reshape(B, L, GDN_HEADS, GDN_DK))
    gk = _l2norm(qkv[..., GDN_QK:2 * GDN_QK].reshape(B, L, GDN_HEADS, GDN_DK))
    gv = qkv[..., 2 * GDN_QK:].reshape(B, L, GDN_HEADS, GDN_DV)
    a_in = proj[..., OFF_GDN_A:OFF_GDN_B].astype(f32)
    b_in = proj[..., OFF_GDN_B:OFF_GDN_Z].astype(f32)
    g = -jnp.exp(gdn_a_log.astype(f32)) * jax.nn.softplus(a_in + gdn_dt_bias.astype(f32))
    beta = jax.nn.sigmoid(b_in)
    go, s_new = _gated_delta_rule(gq, gk, gv, g, beta, s0)
    z = proj[..., OFF_GDN_Z:OFF_SB].reshape(B, L, GDN_HEADS, GDN_DV)
    go = (_rmsnorm(go.astype(x.dtype), gdn_norm_g) * jax.nn.silu(z)).reshape(B, L, GDN_V)
    branch_a = go @ w_br_gdn

    sq = proj[..., OFF_SB:OFF_SB + SB_W].reshape(B, L, SB_HEADS, SB_DH)
    sk = proj[..., OFF_SB + SB_W:OFF_SB + 2 * SB_W].reshape(B, L, SB_HEADS, SB_DH)
    sv = proj[..., OFF_SB + 2 * SB_W:OFF_FOX].reshape(B, L, SB_HEADS, SB_DH)
    fq = proj[..., OFF_FOX:OFF_FOX + FOX_W].reshape(B, L, FOX_HEADS, FOX_DH)
    fk = proj[..., OFF_FOX + FOX_W:OFF_FOX + 2 * FOX_W].reshape(B, L, FOX_HEADS, FOX_DH)
    fv = proj[..., OFF_FOX + 2 * FOX_W:OFF_FOX_F].reshape(B, L, FOX_HEADS, FOX_DH)
    logf = jax.nn.log_sigmoid(proj[..., OFF_FOX_F:OFF_GATE].astype(f32) + fox_f_bias.astype(f32))
    if past is None:
        q_offset = 0
        sk_all, sv_all, fk_all, fv_all, logf_all = sk, sv, fk, fv, logf
    else:
        pk, pv, pfk, pfv, plogf = past
        q_offset = pk.shape[1]
        sk_all = jnp.concatenate([pk.astype(sk.dtype), sk], axis=1)
        sv_all = jnp.concatenate([pv.astype(sv.dtype), sv], axis=1)
        fk_all = jnp.concatenate([pfk.astype(fk.dtype), fk], axis=1)
        fv_all = jnp.concatenate([pfv.astype(fv.dtype), fv], axis=1)
        logf_all = jnp.concatenate([plogf.astype(f32), logf], axis=1)
    branch_b = _stick_breaking(sq, sk_all, sv_all, q_offset).reshape(B, L, SB_W) @ w_br_sb
    branch_c = _forgetting_attention(fq, fk_all, fv_all, logf_all, q_offset).reshape(B, L, FOX_W) @ w_br_fox

    gates = jax.nn.sigmoid(proj[..., OFF_GATE:]).reshape(B, L, N_BRANCH, D_MODEL)
    merged = gates[..., 0, :] * branch_a + gates[..., 1, :] * branch_b + gates[..., 2, :] * branch_c
    x = x + gt1 * (merged @ w_out)

    u2 = _rmsnorm(x, norm2_g) * (1.0 + sc2) + sh2
    up = u2 @ w_up
    hc, ffn_buf_new = _causal_dwconv(up, ffn_buf, ffn_conv_w)
    hc = hc + ffn_conv_b
    hid = jax.nn.silu(hc[..., :D_FF]) * hc[..., D_FF:]
    x = x + gt2 * (hid @ w_down)
    return x, (sk, sv, fk, fv, logf.astype(x.dtype), s_new.astype(x.dtype), gdn_buf_new, ffn_buf_new)


def setup_inputs(seed: int = 0) -> dict:
    key = jax.random.key(seed)
    ks = iter(jax.random.split(key, 48))
    f32 = jnp.float32
    n_pages = PAST_LEN // PAGE_SIZE
    n_used = DEC_BATCH * n_pages
    n_pool = n_used + (n_used + 3) // 4

    def nrm(shape, scale):
        return jax.random.normal(next(ks), shape, f32) * scale

    x_prompt = nrm((BATCH, SEQ, D_MODEL), 1.0)
    x_sample = nrm((DEC_BATCH, DEC_SEQ, D_MODEL), 1.0)
    cache_sb_k = nrm((DEPTH, n_pool, PAGE_SIZE, SB_HEADS, SB_DH), 1.0)
    cache_sb_v = nrm((DEPTH, n_pool, PAGE_SIZE, SB_HEADS, SB_DH), 1.0)
    cache_fox_k = nrm((DEPTH, n_pool, PAGE_SIZE, FOX_HEADS, FOX_DH), 1.0)
    cache_fox_v = nrm((DEPTH, n_pool, PAGE_SIZE, FOX_HEADS, FOX_DH), 1.0)
    cache_fox_logf = jax.nn.log_sigmoid(FOX_F_BIAS_INIT + nrm((DEPTH, n_pool, PAGE_SIZE, FOX_HEADS), 0.5))
    state_gdn = nrm((DEPTH, DEC_BATCH, GDN_HEADS, GDN_DK, GDN_DV), GDN_DK ** -0.5)
    state_gdn_conv = nrm((DEPTH, DEC_BATCH, GDN_CONV - 1, GDN_CONV_CH), 1.0)
    state_ffn_conv = nrm((DEPTH, DEC_BATCH, FFN_CONV - 1, 2 * D_FF), 1.0)
    page_table = jax.random.permutation(next(ks), n_pool)[:n_used].reshape(DEC_BATCH, n_pages).astype(jnp.int32)
    c_prompt = nrm((BATCH, D_MODEL), 1.0)
    c_sample = nrm((DEC_BATCH, D_MODEL), 1.0)

    norm1_g = 1.0 + nrm((DEPTH, D_MODEL), 0.02)
    norm2_g = 1.0 + nrm((DEPTH, D_MODEL), 0.02)
    final_g = 1.0 + nrm((D_MODEL,), 0.02)
    w_ada = nrm((DEPTH, D_MODEL, 6 * D_MODEL), 0.5 * D_MODEL ** -0.5)
    b_ada = nrm((DEPTH, 6 * D_MODEL), 0.01)
    w_in = nrm((DEPTH, D_MODEL, D_IN), D_MODEL ** -0.5)
    gdn_conv_w = nrm((DEPTH, GDN_CONV, GDN_CONV_CH), GDN_CONV ** -0.5)
    gdn_a_log = jnp.log(jax.random.uniform(next(ks), (DEPTH, GDN_HEADS), f32, 1.0, 16.0))
    dt = jnp.exp(jax.random.uniform(next(ks), (DEPTH, GDN_HEADS), f32, math.log(1e-3), math.log(1e-1)))
    gdn_dt_bias = dt + jnp.log(-jnp.expm1(-dt))
    gdn_norm_g = 1.0 + nrm((DEPTH, GDN_DV), 0.02)
    fox_f_bias = FOX_F_BIAS_INIT + nrm((DEPTH, FOX_HEADS), 0.5)
    w_br_gdn = nrm((DEPTH, GDN_V, D_MODEL), GDN_V ** -0.5)
    w_br_sb = nrm((DEPTH, SB_W, D_MODEL), SB_W ** -0.5)
    w_br_fox = nrm((DEPTH, FOX_W, D_MODEL), FOX_W ** -0.5)
    w_out = nrm((DEPTH, D_MODEL, D_MODEL), D_MODEL ** -0.5)
    w_up = nrm((DEPTH, D_MODEL, 2 * D_FF), D_MODEL ** -0.5)
    ffn_conv_w = nrm((DEPTH, FFN_CONV, 2 * D_FF), FFN_CONV ** -0.5)
    ffn_conv_b = nrm((DEPTH, 2 * D_FF), 0.01)
    w_down = nrm((DEPTH, D_FF, D_MODEL), D_FF ** -0.5)
    return {"x_prompt": x_prompt, "x_sample": x_sample,
            "cache_sb_k": cache_sb_k, "cache_sb_v": cache_sb_v,
            "cache_fox_k": cache_fox_k, "cache_fox_v": cache_fox_v, "cache_fox_logf": cache_fox_logf,
            "state_gdn": state_gdn, "state_gdn_conv": state_gdn_conv, "state_ffn_conv": state_ffn_conv,
            "page_table": page_table, "c_prompt": c_prompt, "c_sample": c_sample,
            "norm1_g": norm1_g, "norm2_g": norm2_g, "final_g": final_g,
            "w_ada": w_ada, "b_ada": b_ada, "w_in": w_in, "gdn_conv_w": gdn_conv_w,
            "gdn_a_log": gdn_a_log, "gdn_dt_bias": gdn_dt_bias, "gdn_norm_g": gdn_norm_g,
            "fox_f_bias": fox_f_bias, "w_br_gdn": w_br_gdn, "w_br_sb": w_br_sb, "w_br_fox": w_br_fox,
            "w_out": w_out, "w_up": w_up, "ffn_conv_w": ffn_conv_w, "ffn_conv_b": ffn_conv_b,
            "w_down": w_down}


def reference(x_prompt, x_sample, cache_sb_k, cache_sb_v, cache_fox_k, cache_fox_v, cache_fox_logf,
              state_gdn, state_gdn_conv, state_ffn_conv, page_table, c_prompt, c_sample,
              norm1_g, norm2_g, final_g, w_ada, b_ada, w_in, gdn_conv_w, gdn_a_log, gdn_dt_bias,
              gdn_norm_g, fox_f_bias, w_br_gdn, w_br_sb, w_br_fox, w_out, w_up, ffn_conv_w,
              ffn_conv_b, w_down):
    xp, xs = x_prompt, x_sample
    bp = x_prompt.shape[0]
    new_p, new_s = [], []
    for l in range(DEPTH):
        lw = (norm1_g[l], norm2_g[l], w_ada[l], b_ada[l], w_in[l], gdn_conv_w[l], gdn_a_log[l],
              gdn_dt_bias[l], gdn_norm_g[l], fox_f_bias[l], w_br_gdn[l], w_br_sb[l], w_br_fox[l],
              w_out[l], w_up[l], ffn_conv_w[l], ffn_conv_b[l], w_down[l])
        s0 = jnp.zeros((bp, GDN_HEADS, GDN_DK, GDN_DV), jnp.float32)
        gbuf0 = jnp.zeros((bp, GDN_CONV - 1, GDN_CONV_CH), x_prompt.dtype)
        fbuf0 = jnp.zeros((bp, FFN_CONV - 1, 2 * D_FF), x_prompt.dtype)
        xp, st = _layer(xp, c_prompt, None, s0, gbuf0, fbuf0, *lw)
        new_p.append(st)
        past = (_gather_pages(cache_sb_k[l], page_table), _gather_pages(cache_sb_v[l], page_table),
                _gather_pages(cache_fox_k[l], page_table), _gather_pages(cache_fox_v[l], page_table),
                _gather_pages(cache_fox_logf[l], page_table))
        xs, st = _layer(xs, c_sample, past, state_gdn[l], state_gdn_conv[l], state_ffn_conv[l], *lw)
        new_s.append(st)
    y_prompt = _rmsnorm(xp, final_g)
    y_sample = _rmsnorm(xs, final_g)
    P = [jnp.stack([st[i] for st in new_p]) for i in range(8)]
    S = [jnp.stack([st[i] for st in new_s]) for i in range(8)]
    return (y_prompt, y_sample, P[0], S[0], P[1], S[1], P[2], S[2], P[3], S[3],
            P[4], S[4], P[5], S[5], P[6], S[6], P[7], S[7])
```

```python
import functools
import math

import jax
import jax.numpy as jnp
from jax import lax
from jax.experimental import pallas as pl
from jax.experimental.pallas import tpu as pltpu

F32 = jnp.float32
BF16 = jnp.bfloat16

GDN_HEADS = 4
GDN_DK = 128
GDN_DV = 128
GDN_CONV = 4
GDN_CHUNK = 64
SB_HEADS = 8
SB_DH = 64
FOX_HEADS = 8
FOX_DH = 64
FFN_CONV = 3
EPS = 1e-6
PAGE = 128

GDN_QK = GDN_HEADS * GDN_DK
GDN_V = GDN_HEADS * GDN_DV
GDN_CH = 2 * GDN_QK + GDN_V
SB_W = SB_HEADS * SB_DH
FOX_W = FOX_HEADS * FOX_DH

C_QKV = 0
C_Z = 1536
C_SB = 2048
C_FOX = 3584
C_GATE = 5120
SM_F, SM_A, SM_B = 0, 8, 12

LANES = 128
ROWPAD = 8
UNDERFLOW = -104.0
VMEM_LIMIT = 56 * 1024 * 1024


def _cparams(sem):
    return pltpu.CompilerParams(dimension_semantics=sem, vmem_limit_bytes=VMEM_LIMIT)


def _dot(a, b):
    return lax.dot_general(a, b, (((1,), (0,)), ((), ())), preferred_element_type=F32)


def _dot_nt(a, b):
    return lax.dot_general(a, b, (((1,), (1,)), ((), ())), preferred_element_type=F32)


def _dot_tn(a, b):
    return lax.dot_general(a, b, (((0,), (0,)), ((), ())), preferred_element_type=F32)


def _split2(a):
    hi = a.astype(BF16)
    lo = (a - hi.astype(F32)).astype(BF16)
    return hi, lo


def _split3(a):
    hi = a.astype(BF16)
    r = a - hi.astype(F32)
    mid = r.astype(BF16)
    lo = (r - mid.astype(F32)).astype(BF16)
    return hi, mid, lo


def _dot_exact_lhs(e, a):
    hi, mid, lo = _split3(a)
    return _dot(e, hi) + _dot(e, mid) + _dot(e, lo)


def _dot_exact_rhs(a, e):
    hi, mid, lo = _split3(a)
    return _dot(hi, e) + _dot(mid, e) + _dot(lo, e)


def _dot_hp(a, b):
    ah, al = _split2(a)
    bh, bl = _split2(b)
    return _dot(ah, bh) + _dot(ah, bl) + _dot(al, bh)


def _softplus(x):
    return jnp.maximum(x, 0.0) + jnp.log1p(jnp.exp(-jnp.abs(x)))


def _log_sigmoid(x):
    return jnp.minimum(x, 0.0) - jnp.log1p(jnp.exp(-jnp.abs(x)))


def _sigmoid(x):
    return 1.0 / (1.0 + jnp.exp(-x))


def _silu(x):
    return x * _sigmoid(x)


def _iota(shape, dim):
    return lax.broadcasted_iota(jnp.int32, shape, dim)


def _ada_kernel(c_ref, w_ref, b_ref, o_ref):
    c = c_ref[...]
    o_ref[...] = _dot(_silu(c).astype(BF16), w_ref[...].astype(BF16)) + b_ref[...]


def _ada(c_all, w_ada, b_ada):
    r, d = c_all.shape
    n = w_ada.shape[1]
    tn = d
    return pl.pallas_call(
        _ada_kernel,
        out_shape=jax.ShapeDtypeStruct((r, n), F32),
        grid=(n // tn,),
        in_specs=[pl.BlockSpec((r, d), lambda j: (0, 0)),
                  pl.BlockSpec((d, tn), lambda j: (0, j)),
                  pl.BlockSpec((1, tn), lambda j: (0, j))],
        out_specs=pl.BlockSpec((r, tn), lambda j: (0, j)),
        compiler_params=_cparams(("arbitrary",)),
        name="ada_mod",
    )(c_all, w_ada, b_ada.reshape(1, n))


def _mod_rows(ref, rows):
    v = ref[...]
    return v if v.shape[0] == rows else v[0:1]


def _norm_matmul_kernel(x_ref, g_ref, sc_ref, sh_ref, w_ref, o_ref, u_ref):
    @pl.when(pl.program_id(1) == 0)
    def _():
        x = x_ref[...]
        y = x * lax.rsqrt(jnp.mean(x * x, axis=-1, keepdims=True) + EPS) * g_ref[...]
        rows = x.shape[0]
        u_ref[...] = (y * (1.0 + _mod_rows(sc_ref, rows)) + _mod_rows(sh_ref, rows)).astype(BF16)

    o_ref[...] = _dot(u_ref[...], w_ref[...])


def _mod_spec(tm, rows_per_seq, d, ncols_grid):
    if rows_per_seq == ROWPAD:
        shape, imap = (tm, d), (lambda i: i)
    else:
        shape, imap = (ROWPAD, d), (lambda i: (i * tm) // rows_per_seq)
    if ncols_grid:
        return pl.BlockSpec(shape, lambda i, j: (imap(i), 0))
    return pl.BlockSpec(shape, lambda i: (imap(i), 0))


def _norm_matmul(x, g, sc, sh, w, rows_per_seq, tm, tn, name):
    t, d = x.shape
    n = w.shape[1]
    return pl.pallas_call(
        _norm_matmul_kernel,
        out_shape=jax.ShapeDtypeStruct((t, n), F32),
        grid=(t // tm, n // tn),
        in_specs=[pl.BlockSpec((tm, d), lambda i, j: (i, 0)),
                  pl.BlockSpec((1, d), lambda i, j: (0, 0)),
                  _mod_spec(tm, rows_per_seq, d, True),
                  _mod_spec(tm, rows_per_seq, d, True),
                  pl.BlockSpec((d, tn), lambda i, j: (0, j))],
        out_specs=pl.BlockSpec((tm, tn), lambda i, j: (i, j)),
        scratch_shapes=[pltpu.VMEM((tm, d), BF16)],
        compiler_params=_cparams(("parallel", "arbitrary")),
        name=name,
    )(x, g.reshape(1, d), sc, sh, w)


def _gdn_kernel(qkv_ref, sm_ref, z_ref, conv0_ref, s0_ref, cw_ref, hp_ref, ng_ref,
                o_ref, sout_ref, xbuf, s_ref, *, chunk, n_valid):
    c = chunk
    halo = GDN_CONV - 1

    @pl.when(pl.program_id(1) == 0)
    def _():
        xbuf[0:ROWPAD, :] = conv0_ref[0]
        s_ref[...] = s0_ref[0]

    xbuf[ROWPAD:ROWPAD + c, :] = qkv_ref[...]
    cw = cw_ref[...]
    y = xbuf[ROWPAD - halo:ROWPAD - halo + c, :] * cw[0:1]
    for i in range(1, GDN_CONV):
        y = y + xbuf[ROWPAD - halo + i:ROWPAD - halo + i + c, :] * cw[i:i + 1]
    xbuf[ROWPAD - halo:ROWPAD, :] = xbuf[ROWPAD - halo + c:ROWPAD + c, :]
    y = _silu(y)

    sm = sm_ref[...]
    hp = hp_ref[...]
    gmat = hp[0:1] * _softplus(sm + hp[1:2])
    beta = _sigmoid(sm)
    if n_valid < c:
        live = _iota((c, LANES), 0) < n_valid
        gmat = jnp.where(live, gmat, 0.0)
        beta = jnp.where(live, beta, 0.0)

    ri = _iota((c, c), 0)
    ci = _iota((c, c), 1)
    incl = ri >= ci
    tri = jnp.where(incl, 1.0, 0.0).astype(BF16)
    ones = jnp.ones((c, c), BF16)
    eye = jnp.where(ri == ci, 1.0, 0.0)
    gc_all = _dot_exact_lhs(tri, gmat)
    nsteps = max(1, int(math.ceil(math.log2(c))))

    for h in range(GDN_HEADS):
        ls = slice(h * GDN_DK, (h + 1) * GDN_DK)
        qh = y[:, ls]
        kh = y[:, GDN_QK + h * GDN_DK:GDN_QK + (h + 1) * GDN_DK]
        vh = y[:, 2 * GDN_QK + h * GDN_DV:2 * GDN_QK + (h + 1) * GDN_DV]
        qh = qh * lax.rsqrt(jnp.sum(qh * qh, axis=-1, keepdims=True) + EPS) * (GDN_DK ** -0.5)
        kh = kh * lax.rsqrt(jnp.sum(kh * kh, axis=-1, keepdims=True) + EPS)
        g_col = gmat[:, SM_A + h:SM_A + h + 1]
        b_col = beta[:, SM_B + h:SM_B + h + 1]
        gc = gc_all[:, SM_A + h:SM_A + h + 1]
        gr = _dot_exact_lhs(ones, jnp.where(ri <= ci, g_col, 0.0))
        decay = jnp.where(incl, jnp.exp(jnp.where(incl, gc - gr, 0.0)), 0.0)
        eg = jnp.exp(gc)
        kb = kh * b_col
        vb = vh * b_col
        kh_b = kh.astype(BF16)
        kkt = _dot_nt(kb.astype(BF16), kh_b) * decay
        p = -jnp.where(ri > ci, kkt, 0.0)
        tmat = eye + p
        for _ in range(nsteps - 1):
            p = _dot_hp(p, p)
            tmat = tmat + _dot_hp(tmat, p)
        t_b = tmat.astype(BF16)
        u = _dot(t_b, vb.astype(BF16))
        w = _dot(t_b, (kb * eg).astype(BF16))
        s_old = s_ref[h]
        s_b = s_old.astype(BF16)
        v_new = u - _dot(w.astype(BF16), s_b)
        v_new_b = v_new.astype(BF16)
        qk = _dot_nt(qh.astype(BF16), kh_b) * decay
        o = _dot((qh * eg).astype(BF16), s_b) + _dot(qk.astype(BF16), v_new_b)
        g_last = gc[c - 1:c, :]
        k_dec = kh * jnp.exp(g_last - gc)
        s_ref[h] = s_old * jnp.exp(g_last) + _dot_tn(k_dec.astype(BF16), v_new_b)
        on = o * lax.rsqrt(jnp.mean(o * o, axis=-1, keepdims=True) + EPS) * ng_ref[...]
        zh = z_ref[:, h * GDN_DV:(h + 1) * GDN_DV]
        o_ref[:, h * GDN_DV:(h + 1) * GDN_DV] = on * _silu(zh)

    sout_ref[0] = s_ref[...]


def _gdn(proj, conv0, s0, conv_w, a_log, dt_bias, norm_g, n_seq, rows_per_seq, chunk, n_valid):
    t = proj.shape[0]
    nc = rows_per_seq // chunk
    hp = jnp.zeros((ROWPAD, LANES), F32)
    hp = hp.at[0, SM_A:SM_A + GDN_HEADS].set(-jnp.exp(a_log))
    hp = hp.at[1, SM_A:SM_A + GDN_HEADS].set(dt_bias)
    kern = functools.partial(_gdn_kernel, chunk=chunk, n_valid=n_valid)
    return pl.pallas_call(
        kern,
        out_shape=(jax.ShapeDtypeStruct((t, GDN_V), F32),
                   jax.ShapeDtypeStruct((n_seq, GDN_HEADS, GDN_DK, GDN_DV), F32)),
        grid=(n_seq, nc),
        in_specs=[pl.BlockSpec((chunk, GDN_CH), lambda b, c: (b * nc + c, C_QKV // GDN_CH)),
                  pl.BlockSpec((chunk, LANES), lambda b, c: (b * nc + c, proj.shape[1] // LANES - 1)),
                  pl.BlockSpec((chunk, GDN_V), lambda b, c: (b * nc + c, C_Z // GDN_V)),
                  pl.BlockSpec((1, ROWPAD, GDN_CH), lambda b, c: (b, 0, 0)),
                  pl.BlockSpec((1, GDN_HEADS, GDN_DK, GDN_DV), lambda b, c: (b, 0, 0, 0)),
                  pl.BlockSpec((GDN_CONV, GDN_CH), lambda b, c: (0, 0)),
                  pl.BlockSpec((ROWPAD, LANES), lambda b, c: (0, 0)),
                  pl.BlockSpec((1, GDN_DV), lambda b, c: (0, 0))],
        out_specs=(pl.BlockSpec((chunk, GDN_V), lambda b, c: (b * nc + c, 0)),
                   pl.BlockSpec((1, GDN_HEADS, GDN_DK, GDN_DV), lambda b, c: (b, 0, 0, 0))),
        scratch_shapes=[pltpu.VMEM((ROWPAD + chunk, GDN_CH), F32),
                        pltpu.VMEM((GDN_HEADS, GDN_DK, GDN_DV), F32)],
        compiler_params=_cparams(("arbitrary", "arbitrary")),
        name="gdn_scan",
    )(proj, proj, proj, conv0, s0, conv_w, hp, norm_g.reshape(1, GDN_DV))


def _fox_pre_kernel(sm_ref, b_ref, logf_ref, f_ref, ft_ref, carry):
    @pl.when(pl.program_id(1) == 0)
    def _():
        carry[...] = jnp.zeros_like(carry)

    tm = sm_ref.shape[0]
    lf = _log_sigmoid(sm_ref[...] + b_ref[...])
    logf_ref[...] = lf
    tri = jnp.where(_iota((tm, tm), 0) >= _iota((tm, tm), 1), 1.0, 0.0).astype(BF16)
    cs = _dot_exact_lhs(tri, lf) + carry[...]
    f_ref[...] = cs
    carry[...] = cs[tm - 1:tm, :]
    sel = jnp.where(_iota((ROWPAD, LANES), 0) == _iota((ROWPAD, LANES), 1), 1.0, 0.0).astype(BF16)
    hi, mid, lo = _split3(cs)
    ft_ref[0] = _dot_nt(sel, hi) + _dot_nt(sel, mid) + _dot_nt(sel, lo)


def _fox_pre(proj, f_bias, n_seq, rows_per_seq, tm):
    t = proj.shape[0]
    nb = rows_per_seq // tm
    brow = jnp.zeros((1, LANES), F32).at[0, SM_F:SM_F + FOX_HEADS].set(f_bias)
    return pl.pallas_call(
        _fox_pre_kernel,
        out_shape=(jax.ShapeDtypeStruct((t, LANES), F32),
                   jax.ShapeDtypeStruct((t, LANES), F32),
                   jax.ShapeDtypeStruct((n_seq, ROWPAD, rows_per_seq), F32)),
        grid=(n_seq, nb),
        in_specs=[pl.BlockSpec((tm, LANES), lambda b, i: (b * nb + i, proj.shape[1] // LANES - 1)),
                  pl.BlockSpec((1, LANES), lambda b, i: (0, 0))],
        out_specs=(pl.BlockSpec((tm, LANES), lambda b, i: (b * nb + i, 0)),
                   pl.BlockSpec((tm, LANES), lambda b, i: (b * nb + i, 0)),
                   pl.BlockSpec((1, ROWPAD, tm), lambda b, i: (b, 0, i))),
        scratch_shapes=[pltpu.VMEM((1, LANES), F32)],
        compiler_params=_cparams(("arbitrary", "arbitrary")),
        name="fox_pre",
    )(proj, brow)


def _head_mask(h, dh):
    return (_iota((1, LANES), 1) // dh) == h


def _sb_prompt_kernel(q_ref, k_ref, v_ref, u_ref, o_ref, *, tq, tk):
    i = pl.program_id(2)
    nd = tq // tk
    q2 = q_ref[...] * (SB_DH ** -0.5)
    masks = [_head_mask(h, SB_DH) for h in range(2)]
    qhs = [jnp.where(m, q2, 0.0).astype(BF16) for m in masks]
    ustrict = u_ref[...]

    def block(j, carries, acc, masked):
        start = pl.multiple_of(j * tk, tk)
        kb = k_ref[pl.ds(start, tk), :].astype(BF16)
        vfull = v_ref[pl.ds(start, tk), :]
        if masked:
            earlier = (j * tk + _iota((tq, tk), 1)) < (i * tq + _iota((tq, tk), 0))
        out_c = []
        for h in range(2):
            s = _dot_nt(qhs[h], kb)
            sp = _softplus(s)
            lf = -sp
            if masked:
                lf = jnp.where(earlier, lf, 0.0)
            hi, lo = _split2(lf)
            cum = _dot(hi, ustrict) + _dot(lo, ustrict)
            a = jnp.exp(s - sp + carries[h] + cum)
            if masked:
                a = jnp.where(earlier, a, 0.0)
            vb = jnp.where(masks[h], vfull, 0.0).astype(BF16)
            acc = acc + _dot(a.astype(BF16), vb)
            out_c.append(carries[h] + cum[:, 0:1] + lf[:, 0:1])
        return out_c, acc

    carries = [jnp.zeros((tq, 1), F32), jnp.zeros((tq, 1), F32)]
    acc = jnp.zeros((tq, LANES), F32)
    for d in range(nd):
        carries, acc = block(i * nd + (nd - 1 - d), carries, acc, True)

    def cond(st):
        j, c0, c1, _ = st
        return jnp.logical_and(j >= 0, jnp.maximum(jnp.max(c0), jnp.max(c1)) > UNDERFLOW)

    def body(st):
        j, c0, c1, a = st
        (c0, c1), a = block(j, [c0, c1], a, False)
        return j - 1, c0, c1, a

    st = lax.while_loop(cond, body, (i * nd - 1, carries[0], carries[1], acc))
    o_ref[...] = st[3]


def _sb_prompt(proj, n_seq, seq, tq, tk):
    t = proj.shape[0]
    nq = seq // tq
    cq, ck, cv = C_SB // LANES, (C_SB + SB_W) // LANES, (C_SB + 2 * SB_W) // LANES
    ustrict = (jnp.arange(tk)[:, None] > jnp.arange(tk)[None, :]).astype(BF16)
    return pl.pallas_call(
        functools.partial(_sb_prompt_kernel, tq=tq, tk=tk),
        out_shape=jax.ShapeDtypeStruct((t, SB_W), F32),
        grid=(n_seq, SB_HEADS // 2, nq),
        in_specs=[pl.BlockSpec((tq, LANES), lambda b, p, i: (b * nq + i, cq + p)),
                  pl.BlockSpec((seq, LANES), lambda b, p, i: (b, ck + p)),
                  pl.BlockSpec((seq, LANES), lambda b, p, i: (b, cv + p)),
                  pl.BlockSpec((tk, tk), lambda b, p, i: (0, 0))],
        out_specs=pl.BlockSpec((tq, LANES), lambda b, p, i: (b * nq + i, p)),
        compiler_params=_cparams(("parallel", "parallel", "arbitrary")),
        name="sb_prompt",
    )(proj, proj, proj, ustrict)


def _fox_prompt_kernel(q_ref, k_ref, v_ref, f_ref, ft_ref, o_ref, *, tq, tk):
    p = pl.program_id(1)
    i = pl.program_id(2)
    nd = tq // tk
    q2 = q_ref[...] * (FOX_DH ** -0.5)
    masks = [_head_mask(h, FOX_DH) for h in range(2)]
    qhs = [jnp.where(m, q2, 0.0).astype(BF16) for m in masks]
    fblk = f_ref[...]
    lane = _iota((1, LANES), 1)
    fq = [jnp.sum(jnp.where(lane == 2 * p + h, fblk, 0.0), axis=1, keepdims=True) for h in range(2)]

    def block(j, m, l, acc, masked):
        start = pl.multiple_of(j * tk, tk)
        kb = k_ref[pl.ds(start, tk), :].astype(BF16)
        vfull = v_ref[pl.ds(start, tk), :]
        if masked:
            visible = (j * tk + _iota((tq, tk), 1)) <= (i * tq + _iota((tq, tk), 0))
        ftb = ft_ref[0, :, pl.ds(start, tk)]
        hrow = _iota((ROWPAD, tk), 0)
        m2, l2, a2 = [], [], []
        for h in range(2):
            fk = jnp.sum(jnp.where(hrow == 2 * p + h, ftb, 0.0), axis=0, keepdims=True)
            s = _dot_nt(qhs[h], kb) + (fq[h] - fk)
            if masked:
                s = jnp.where(visible, s, -jnp.inf)
            m_new = jnp.maximum(m[h], jnp.max(s, axis=1, keepdims=True))
            alpha = jnp.exp(m[h] - m_new)
            pr = jnp.exp(s - m_new)
            vb = jnp.where(masks[h], vfull, 0.0).astype(BF16)
            l2.append(l[h] * alpha + jnp.sum(pr, axis=1, keepdims=True))
            a2.append(acc[h] * alpha + _dot(pr.astype(BF16), vb))
            m2.append(m_new)
        return m2, l2, a2

    def body(j, st):
        m0, m1, l0, l1, a0, a1 = st
        m, l, a = block(j, [m0, m1], [l0, l1], [a0, a1], False)
        return m[0], m[1], l[0], l[1], a[0], a[1]

    neg = jnp.full((tq, 1), -jnp.inf, F32)
    zc = jnp.zeros((tq, 1), F32)
    za = jnp.zeros((tq, LANES), F32)
    st = lax.fori_loop(0, i * nd, body, (neg, neg, zc, zc, za, za))
    m, l, a = [st[0], st[1]], [st[2], st[3]], [st[4], st[5]]
    for d in range(nd):
        m, l, a = block(i * nd + d, m, l, a, True)
    o_ref[...] = a[0] * (1.0 / l[0]) + a[1] * (1.0 / l[1])


def _fox_prompt(proj, fsum, fsum_t, n_seq, seq, tq, tk):
    t = proj.shape[0]
    nq = seq // tq
    cq, ck, cv = C_FOX // LANES, (C_FOX + FOX_W) // LANES, (C_FOX + 2 * FOX_W) // LANES
    return pl.pallas_call(
        functools.partial(_fox_prompt_kernel, tq=tq, tk=tk),
        out_shape=jax.ShapeDtypeStruct((t, FOX_W), F32),
        grid=(n_seq, FOX_HEADS // 2, nq),
        in_specs=[pl.BlockSpec((tq, LANES), lambda b, p, i: (b * nq + i, cq + p)),
                  pl.BlockSpec((seq, LANES), lambda b, p, i: (b, ck + p)),
                  pl.BlockSpec((seq, LANES), lambda b, p, i: (b, cv + p)),
                  pl.BlockSpec((tq, LANES), lambda b, p, i: (b * nq + i, 0)),
                  pl.BlockSpec((1, ROWPAD, seq), lambda b, p, i: (b, 0, 0))],
        out_specs=pl.BlockSpec((tq, LANES), lambda b, p, i: (b * nq + i, p)),
        compiler_params=_cparams(("parallel", "parallel", "arbitrary")),
        name="fox_prompt",
    )(proj, proj, proj, fsum, fsum_t)


def _suffix_scan(x, heads, inclusive_total=False):
    n = x.shape[-1]
    lane = _iota(x.shape, len(x.shape) - 1)
    acc = x
    d = heads
    while d < n:
        shifted = pltpu.roll(acc, n - d, len(x.shape) - 1)
        acc = acc + jnp.where(lane < n - d, shifted, 0.0)
        d *= 2
    return acc - x, acc


def _class_total(x, heads):
    n = x.shape[-1]
    acc = x
    d = heads
    while d < n:
        acc = acc + pltpu.roll(acc, d, len(x.shape) - 1)
        d *= 2
    return acc


def _sb_decode_kernel(pt_ref, q_ref, kn_ref, vn_ref, k_ref, v_ref, o_ref, carry, acc, live, *, n_new):
    j = pl.program_id(1)
    heads = SB_HEADS
    rows = heads * ROWPAD
    qf = (q_ref[0] * (SB_DH ** -0.5)).astype(BF16)

    @pl.when(j == 0)
    def _():
        ncol = ROWPAD * heads
        r = _iota((rows, ncol), 0)
        c = _iota((rows, ncol), 1)
        ok = jnp.logical_and((c % heads) == (r // ROWPAD), (c // heads) < (r % ROWPAD))
        s = _dot_nt(qf, kn_ref[0].astype(BF16))
        sp = _softplus(s)
        lf = jnp.where(ok, -sp, 0.0)
        later = jnp.where(_iota((ncol, ncol), 0) > _iota((ncol, ncol), 1), 1.0, 0.0).astype(BF16)
        cum = _dot_exact_rhs(lf, later)
        a = jnp.where(ok, jnp.exp(s - sp + cum), 0.0)
        acc[...] = _dot(a.astype(BF16), vn_ref[0].astype(BF16))
        tot = jnp.sum(lf, axis=1, keepdims=True)
        carry[...] = tot
        live[0] = (jnp.max(tot) > UNDERFLOW).astype(jnp.int32)

    @pl.when(jnp.logical_and(j > 0, live[0] > 0))
    def _():
        ncol = PAGE * heads
        kf = k_ref[0, 0].reshape(ncol, SB_DH).astype(BF16)
        vf = v_ref[0, 0].reshape(ncol, SB_DH).astype(BF16)
        ok = (_iota((rows, ncol), 1) % heads) == (_iota((rows, ncol), 0) // ROWPAD)
        s = _dot_nt(qf, kf)
        sp = _softplus(s)
        lf = jnp.where(ok, -sp, 0.0)
        cum, _ = _suffix_scan(lf, heads)
        cr = carry[...]
        a = jnp.where(ok, jnp.exp(s - sp + cr + cum), 0.0)
        acc[...] += _dot(a.astype(BF16), vf)
        tot = cr + jnp.sum(lf, axis=1, keepdims=True)
        carry[...] = tot
        live[0] = (jnp.max(tot) > UNDERFLOW).astype(jnp.int32)

    o_ref[0] = acc[...]


def _page_map(layer, n_pages):
    def imap(b, j, pt):
        return (layer, pt[b, n_pages - jnp.maximum(j, 1)], 0, 0, 0)
    return imap


def _sb_decode(page_table, qf, kn, vn, cache_k, cache_v, layer, n_new):
    n_seq, n_pages = page_table.shape
    rows = SB_HEADS * ROWPAD
    blk = (1, 1, PAGE, SB_HEADS, SB_DH)
    seq_spec = pl.BlockSpec((1, rows, SB_DH), lambda b, j, pt: (b, 0, 0))
    grid_spec = pltpu.PrefetchScalarGridSpec(
        num_scalar_prefetch=1,
        grid=(n_seq, n_pages + 1),
        in_specs=[seq_spec, seq_spec, seq_spec,
                  pl.BlockSpec(blk, _page_map(layer, n_pages)),
                  pl.BlockSpec(blk, _page_map(layer, n_pages))],
        out_specs=seq_spec,
        scratch_shapes=[pltpu.VMEM((rows, 1), F32), pltpu.VMEM((rows, SB_DH), F32),
                        pltpu.SMEM((1,), jnp.int32)],
    )
    return pl.pallas_call(
        functools.partial(_sb_decode_kernel, n_new=n_new),
        out_shape=jax.ShapeDtypeStruct((n_seq, rows, SB_DH), F32),
        grid_spec=grid_spec,
        compiler_params=_cparams(("arbitrary", "arbitrary")),
        name="sb_decode",
    )(page_table, qf, kn, vn, cache_k, cache_v)


def _fox_decode_kernel(pt_ref, q_ref, kn_ref, vn_ref, pc_ref, pr_ref, k_ref, v_ref, lf_ref, o_ref,
                       m_ref, l_ref, acc, dsum):
    j = pl.program_id(1)
    heads = FOX_HEADS
    rows = heads * ROWPAD
    qf = (q_ref[0] * (FOX_DH ** -0.5)).astype(BF16)
    pcol = pc_ref[0]

    @pl.when(j == 0)
    def _():
        ncol = ROWPAD * heads
        r = _iota((rows, ncol), 0)
        c = _iota((rows, ncol), 1)
        ok = jnp.logical_and((c % heads) == (r // ROWPAD), (c // heads) <= (r % ROWPAD))
        s = _dot_nt(qf, kn_ref[0].astype(BF16)) + (pcol - pr_ref[0])
        s = jnp.where(ok, s, -jnp.inf)
        m = jnp.max(s, axis=1, keepdims=True)
        pr = jnp.exp(s - m)
        m_ref[...] = m
        l_ref[...] = jnp.sum(pr, axis=1, keepdims=True)
        acc[...] = _dot(pr.astype(BF16), vn_ref[0].astype(BF16))
        dsum[...] = jnp.zeros_like(dsum)

    @pl.when(j > 0)
    def _():
        ncol = PAGE * heads
        kf = k_ref[0, 0].reshape(ncol, FOX_DH).astype(BF16)
        vf = v_ref[0, 0].reshape(ncol, FOX_DH).astype(BF16)
        lfp = jnp.broadcast_to(lf_ref[0, 0], (ROWPAD, ncol))
        later, _ = _suffix_scan(lfp, heads)
        dprev = dsum[...]
        bias = (dprev + later)[0:1, :]
        dsum[...] = dprev + _class_total(lfp, heads)
        ok = (_iota((rows, ncol), 1) % heads) == (_iota((rows, ncol), 0) // ROWPAD)
        s = _dot_nt(qf, kf) + (pcol + bias)
        s = jnp.where(ok, s, -jnp.inf)
        m_old = m_ref[...]
        m_new = jnp.maximum(m_old, jnp.max(s, axis=1, keepdims=True))
        alpha = jnp.exp(m_old - m_new)
        pr = jnp.exp(s - m_new)
        l_ref[...] = l_ref[...] * alpha + jnp.sum(pr, axis=1, keepdims=True)
        acc[...] = acc[...] * alpha + _dot(pr.astype(BF16), vf)
        m_ref[...] = m_new

    o_ref[0] = acc[...] * (1.0 / l_ref[...])


def _fox_decode(page_table, qf, kn, vn, pcol, prow, cache_k, cache_v, cache_lf, layer):
    n_seq, n_pages = page_table.shape
    rows = FOX_HEADS * ROWPAD
    ncol = PAGE * FOX_HEADS
    blk = (1, 1, PAGE, FOX_HEADS, FOX_DH)
    seq_spec = pl.BlockSpec((1, rows, FOX_DH), lambda b, j, pt: (b, 0, 0))

    def lf_map(b, j, pt):
        return (layer, pt[b, n_pages - jnp.maximum(j, 1)], 0, 0)

    grid_spec = pltpu.PrefetchScalarGridSpec(
        num_scalar_prefetch=1,
        grid=(n_seq, n_pages + 1),
        in_specs=[seq_spec, seq_spec, seq_spec,
                  pl.BlockSpec((1, rows, 1), lambda b, j, pt: (b, 0, 0)),
                  pl.BlockSpec((1, 1, rows), lambda b, j, pt: (b, 0, 0)),
                  pl.BlockSpec(blk, _page_map(layer, n_pages)),
                  pl.BlockSpec(blk, _page_map(layer, n_pages)),
                  pl.BlockSpec((1, 1, 1, ncol), lf_map)],
        out_specs=seq_spec,
        scratch_shapes=[pltpu.VMEM((rows, 1), F32), pltpu.VMEM((rows, 1), F32),
                        pltpu.VMEM((rows, FOX_DH), F32), pltpu.VMEM((ROWPAD, ncol), F32)],
    )
    return pl.pallas_call(
        _fox_decode_kernel,
        out_shape=jax.ShapeDtypeStruct((n_seq, rows, FOX_DH), F32),
        grid_spec=grid_spec,
        compiler_params=_cparams(("arbitrary", "arbitrary")),
        name="fox_decode",
    )(page_table, qf, kn, vn, pcol, prow, cache_k, cache_v, cache_lf)


def _merge_kernel(x_ref, go_ref, sb_ref, fx_ref, g0_ref, g1_ref, g2_ref, gt_ref,
                  wg_ref, ws_ref, wf_ref, wo_ref, o_ref):
    rows = x_ref.shape[0]
    a = _dot(go_ref[...].astype(BF16), wg_ref[...])
    b = _dot(sb_ref[...].astype(BF16), ws_ref[...])
    c = _dot(fx_ref[...].astype(BF16), wf_ref[...])
    merged = _sigmoid(g0_ref[...]) * a + _sigmoid(g1_ref[...]) * b + _sigmoid(g2_ref[...]) * c
    o_ref[...] = x_ref[...] + _mod_rows(gt_ref, rows) * _dot(merged.astype(BF16), wo_ref[...])


def _merge(x, go, sbo, foxo, proj, gt, wg, ws, wf, wo, rows_per_seq, tm):
    t, d = x.shape
    cg = C_GATE // d
    row = lambda w: pl.BlockSpec((tm, w), lambda i: (i, 0))
    full = lambda a: pl.BlockSpec(a.shape, lambda i: (0, 0))
    return pl.pallas_call(
        _merge_kernel,
        out_shape=jax.ShapeDtypeStruct((t, d), F32),
        grid=(t // tm,),
        in_specs=[row(d), row(GDN_V), row(SB_W), row(FOX_W),
                  pl.BlockSpec((tm, d), lambda i: (i, cg)),
                  pl.BlockSpec((tm, d), lambda i: (i, cg + 1)),
                  pl.BlockSpec((tm, d), lambda i: (i, cg + 2)),
                  _mod_spec(tm, rows_per_seq, d, False),
                  full(wg), full(ws), full(wf), full(wo)],
        out_specs=row(d),
        compiler_params=_cparams(("parallel",)),
        name="merge_out",
    )(x, go, sbo, foxo, proj, proj, proj, gt, wg, ws, wf, wo)


def _ffn_down_kernel(up_ref, prev_ref, x_ref, gt_ref, cw_ref, cb_ref, wd_ref, o_ref, xs, *, cwid):
    tm = up_ref.shape[0]
    d_ff = wd_ref.shape[0]
    halo = FFN_CONV - 1
    xs[0:ROWPAD, :] = prev_ref[0]
    xs[ROWPAD:ROWPAD + tm, :] = up_ref[...]

    def conv(c0):
        cs = slice(c0, c0 + cwid)
        y = cb_ref[:, cs] + xs[ROWPAD - halo:ROWPAD - halo + tm, cs] * cw_ref[0:1, cs]
        for i in range(1, FFN_CONV):
            y = y + xs[ROWPAD - halo + i:ROWPAD - halo + i + tm, cs] * cw_ref[i:i + 1, cs]
        return y

    acc = jnp.zeros((tm, o_ref.shape[1]), F32)
    for c0 in range(0, d_ff, cwid):
        hid = _silu(conv(c0)) * conv(d_ff + c0)
        acc = acc + _dot(hid.astype(BF16), wd_ref[c0:c0 + cwid, :])
    o_ref[...] = x_ref[...] + _mod_rows(gt_ref, tm) * acc


def _ffn_down(up, prev, x, gt, conv_w, conv_b, wd, rows_per_seq, tm):
    t, d = x.shape
    n2 = up.shape[1]
    d_ff = n2 // 2
    cwid = 256 if d_ff % 256 == 0 else LANES
    if rows_per_seq == ROWPAD:
        gt_spec = pl.BlockSpec((tm, d), lambda i: (i, 0))
    else:
        gt_spec = pl.BlockSpec((ROWPAD, d), lambda i: ((i * tm) // rows_per_seq, 0))
    return pl.pallas_call(
        functools.partial(_ffn_down_kernel, cwid=cwid),
        out_shape=jax.ShapeDtypeStruct((t, d), F32),
        grid=(t // tm,),
        in_specs=[pl.BlockSpec((tm, n2), lambda i: (i, 0)),
                  pl.BlockSpec((1, ROWPAD, n2), lambda i: (i, 0, 0)),
                  pl.BlockSpec((tm, d), lambda i: (i, 0)),
                  gt_spec,
                  pl.BlockSpec((FFN_CONV, n2), lambda i: (0, 0)),
                  pl.BlockSpec((1, n2), lambda i: (0, 0)),
                  pl.BlockSpec((d_ff, d), lambda i: (0, 0))],
        out_specs=pl.BlockSpec((tm, d), lambda i: (i, 0)),
        scratch_shapes=[pltpu.VMEM((ROWPAD + tm, n2), F32)],
        compiler_params=_cparams(("parallel",)),
        name="ffn_down",
    )(up, prev, x, gt, conv_w, conv_b.reshape(1, n2), wd)


def _final_norm_kernel(x_ref, g_ref, o_ref):
    x = x_ref[...]
    o_ref[...] = x * lax.rsqrt(jnp.mean(x * x, axis=-1, keepdims=True) + EPS) * g_ref[...]


def _final_norm(x, g, tm):
    t, d = x.shape
    return pl.pallas_call(
        _final_norm_kernel,
        out_shape=jax.ShapeDtypeStruct((t, d), F32),
        grid=(t // tm,),
        in_specs=[pl.BlockSpec((tm, d), lambda i: (i, 0)), pl.BlockSpec((1, d), lambda i: (0, 0))],
        out_specs=pl.BlockSpec((tm, d), lambda i: (i, 0)),
        compiler_params=_cparams(("parallel",)),
        name="final_norm",
    )(x, g.reshape(1, d))


def _tile(n, pref):
    t = min(pref, n)
    while n % t:
        t //= 2
    return t


def _prep_w_in(w_in, d):
    o_a = GDN_CH
    o_b = o_a + GDN_HEADS
    o_z = o_b + GDN_HEADS
    o_sb = o_z + GDN_V
    o_fox = o_sb + 3 * SB_W
    o_f = o_fox + 3 * FOX_W
    o_gate = o_f + FOX_HEADS
    small = jnp.concatenate([w_in[:, o_f:o_gate], w_in[:, o_a:o_b], w_in[:, o_b:o_z]], axis=1)
    small = jnp.pad(small, ((0, 0), (0, LANES - small.shape[1])))
    w = jnp.concatenate([w_in[:, :o_a], w_in[:, o_z:o_sb], w_in[:, o_sb:o_fox], w_in[:, o_fox:o_f],
                         w_in[:, o_gate:], small], axis=1)
    return w.astype(BF16)


def _rep(v):
    return jnp.repeat(v, ROWPAD, axis=0)


def _layer_group(x, mod, lw, rows_per_seq, n_valid, conv0, s0, ffn_prev_fn, attn_fn):
    t, d = x.shape
    n_seq = t // rows_per_seq
    sh1, sc1, gt1, sh2, sc2, gt2 = [_rep(m) for m in jnp.split(mod, 6, axis=-1)]
    tm = _tile(t, 512)
    n_proj = lw["w_in"].shape[1]
    proj = _norm_matmul(x, lw["norm1_g"], sc1, sh1, lw["w_in"], rows_per_seq, tm,
                        n_proj // 5, "proj_in")
    chunk = GDN_CHUNK if rows_per_seq % GDN_CHUNK == 0 else rows_per_seq
    go, s_new = _gdn(proj, conv0, s0, lw["gdn_conv_w"], lw["gdn_a_log"], lw["gdn_dt_bias"],
                     lw["gdn_norm_g"], n_seq, rows_per_seq, chunk, min(n_valid, chunk))
    logf, fsum, fsum_t = _fox_pre(proj, lw["fox_f_bias"], n_seq, rows_per_seq, _tile(rows_per_seq, 256))
    sbo, foxo = attn_fn(proj, fsum, fsum_t)
    x1 = _merge(x, go, sbo, foxo, proj, gt1, lw["w_br_gdn"], lw["w_br_sb"], lw["w_br_fox"], lw["w_out"],
                rows_per_seq, _tile(t, 256))
    n_up = lw["w_up"].shape[1]
    up = _norm_matmul(x1, lw["norm2_g"], sc2, sh2, lw["w_up"], rows_per_seq, tm, n_up // 4, "ffn_up")
    tm_d = ROWPAD if rows_per_seq == ROWPAD else _tile(rows_per_seq, 256)
    x2 = _ffn_down(up, ffn_prev_fn(up, tm_d), x1, gt2, lw["ffn_conv_w"], lw["ffn_conv_b"], lw["w_down"],
                   rows_per_seq, tm_d)
    return x2, proj, up, logf, s_new


def kernel(x_prompt, x_sample, cache_sb_k, cache_sb_v, cache_fox_k, cache_fox_v, cache_fox_logf, state_gdn, state_gdn_conv, state_ffn_conv, page_table, c_prompt, c_sample, norm1_g, norm2_g, final_g, w_ada, b_ada, w_in, gdn_conv_w, gdn_a_log, gdn_dt_bias, gdn_norm_g, fox_f_bias, w_br_gdn, w_br_sb, w_br_fox, w_out, w_up, ffn_conv_w, ffn_conv_b, w_down):
    bp, seq, d = x_prompt.shape
    bs, dseq, _ = x_sample.shape
    depth = w_in.shape[0]
    n2 = w_up.shape[2]
    assert dseq <= ROWPAD and seq % GDN_CHUNK == 0 and dseq >= GDN_CONV - 1

    xp = x_prompt.reshape(bp * seq, d)
    xs = jnp.pad(x_sample, ((0, 0), (0, ROWPAD - dseq), (0, 0))).reshape(bs * ROWPAD, d)
    n_c = bp + bs
    c_all = jnp.pad(jnp.concatenate([c_prompt, c_sample], axis=0), ((0, (-n_c) % ROWPAD), (0, 0)))
    lf_cache = cache_fox_logf.reshape(depth, cache_fox_logf.shape[1], 1, PAGE * FOX_HEADS)

    def to_rows(a, heads, dh):
        return a.reshape(bs, ROWPAD, heads, dh).transpose(0, 2, 1, 3).reshape(bs, heads * ROWPAD, dh)

    def to_page(a, heads, dh):
        return a.reshape(bs, ROWPAD * heads, dh)

    def from_rows(o, heads, dh):
        return o.reshape(bs, heads, ROWPAD, dh).transpose(0, 2, 1, 3).reshape(bs * ROWPAD, heads * dh)

    outs_p, outs_s = [], []
    for l in range(depth):
        lw = {"norm1_g": norm1_g[l], "norm2_g": norm2_g[l], "w_in": _prep_w_in(w_in[l], d),
              "gdn_conv_w": gdn_conv_w[l], "gdn_a_log": gdn_a_log[l], "gdn_dt_bias": gdn_dt_bias[l],
              "gdn_norm_g": gdn_norm_g[l], "fox_f_bias": fox_f_bias[l],
              "w_br_gdn": w_br_gdn[l].astype(BF16), "w_br_sb": w_br_sb[l].astype(BF16),
              "w_br_fox": w_br_fox[l].astype(BF16), "w_out": w_out[l].astype(BF16),
              "w_up": w_up[l].astype(BF16), "ffn_conv_w": ffn_conv_w[l], "ffn_conv_b": ffn_conv_b[l],
              "w_down": w_down[l].astype(BF16)}
        mod = _ada(c_all, w_ada[l], b_ada[l])

        def prompt_attn(proj, fsum, fsum_t):
            tq = _tile(seq, 256)
            tk = _tile(tq, 128)
            return (_sb_prompt(proj, bp, seq, tq, tk), _fox_prompt(proj, fsum, fsum_t, bp, seq, tq, tk))

        def prompt_prev(up, tm_d):
            nt = up.shape[0] // tm_d
            tails = up.reshape(nt, tm_d, n2)[:, tm_d - ROWPAD:, :]
            prev = jnp.concatenate([jnp.zeros((1, ROWPAD, n2), F32), tails[:-1]], axis=0)
            first = (jnp.arange(nt) * tm_d) % seq == 0
            return jnp.where(first[:, None, None], 0.0, prev)

        xp, proj, up, logf, s_new = _layer_group(
            xp, mod[:bp], lw, seq, seq,
            jnp.zeros((bp, ROWPAD, GDN_CH), F32), jnp.zeros((bp, GDN_HEADS, GDN_DK, GDN_DV), F32),
            prompt_prev, prompt_attn)
        p3 = proj.reshape(bp, seq, -1)
        outs_p.append((
            p3[:, :, C_SB + SB_W:C_SB + 2 * SB_W].reshape(bp, seq, SB_HEADS, SB_DH),
            p3[:, :, C_SB + 2 * SB_W:C_SB + 3 * SB_W].reshape(bp, seq, SB_HEADS, SB_DH),
            p3[:, :, C_FOX + FOX_W:C_FOX + 2 * FOX_W].reshape(bp, seq, FOX_HEADS, FOX_DH),
            p3[:, :, C_FOX + 2 * FOX_W:C_FOX + 3 * FOX_W].reshape(bp, seq, FOX_HEADS, FOX_DH),
            logf.reshape(bp, seq, LANES)[:, :, SM_F:SM_F + FOX_HEADS],
            s_new,
            p3[:, seq - (GDN_CONV - 1):, C_QKV:C_QKV + GDN_CH],
            up.reshape(bp, seq, n2)[:, seq - (FFN_CONV - 1):, :]))

        def sample_attn(proj, fsum, fsum_t, l=l):
            def cols(c0, w):
                return proj[:, c0:c0 + w]
            sbo = _sb_decode(page_table, to_rows(cols(C_SB, SB_W), SB_HEADS, SB_DH),
                             to_page(cols(C_SB + SB_W, SB_W), SB_HEADS, SB_DH),
                             to_page(cols(C_SB + 2 * SB_W, SB_W), SB_HEADS, SB_DH),
                             cache_sb_k, cache_sb_v, l, dseq)
            f3 = fsum.reshape(bs, ROWPAD, LANES)[:, :, SM_F:SM_F + FOX_HEADS]
            pcol = f3.transpose(0, 2, 1).reshape(bs, FOX_HEADS * ROWPAD, 1)
            prow = f3.reshape(bs, 1, ROWPAD * FOX_HEADS)
            foxo = _fox_decode(page_table, to_rows(cols(C_FOX, FOX_W), FOX_HEADS, FOX_DH),
                               to_page(cols(C_FOX + FOX_W, FOX_W), FOX_HEADS, FOX_DH),
                               to_page(cols(C_FOX + 2 * FOX_W, FOX_W), FOX_HEADS, FOX_DH),
                               pcol, prow, cache_fox_k, cache_fox_v, lf_cache, l)
            return from_rows(sbo, SB_HEADS, SB_DH), from_rows(foxo, FOX_HEADS, FOX_DH)

        def sample_prev(up, tm_d, l=l):
            return jnp.pad(state_ffn_conv[l], ((0, 0), (ROWPAD - (FFN_CONV - 1), 0), (0, 0)))

        conv0 = jnp.pad(state_gdn_conv[l], ((0, 0), (ROWPAD - (GDN_CONV - 1), 0), (0, 0)))
        xs, proj, up, logf, s_new = _layer_group(
            xs, mod[bp:bp + bs], lw, ROWPAD, dseq, conv0, state_gdn[l], sample_prev, sample_attn)
        p3 = proj.reshape(bs, ROWPAD, -1)
        outs_s.append((
            p3[:, :dseq, C_SB + SB_W:C_SB + 2 * SB_W].reshape(bs, dseq, SB_HEADS, SB_DH),
            p3[:, :dseq, C_SB + 2 * SB_W:C_SB + 3 * SB_W].reshape(bs, dseq, SB_HEADS, SB_DH),
            p3[:, :dseq, C_FOX + FOX_W:C_FOX + 2 * FOX_W].reshape(bs, dseq, FOX_HEADS, FOX_DH),
            p3[:, :dseq, C_FOX + 2 * FOX_W:C_FOX + 3 * FOX_W].reshape(bs, dseq, FOX_HEADS, FOX_DH),
            logf.reshape(bs, ROWPAD, LANES)[:, :dseq, SM_F:SM_F + FOX_HEADS],
            s_new,
            p3[:, dseq - (GDN_CONV - 1):dseq, C_QKV:C_QKV + GDN_CH],
            up.reshape(bs, ROWPAD, n2)[:, dseq - (FFN_CONV - 1):dseq, :]))

    y_p = _final_norm(xp, final_g, _tile(xp.shape[0], 512)).reshape(bp, seq, d)
    y_s = _final_norm(xs, final_g, _tile(xs.shape[0], 512)).reshape(bs, ROWPAD, d)[:, :dseq]
    P = [jnp.stack([st[i] for st in outs_p]) for i in range(8)]
    S = [jnp.stack([st[i] for st in outs_s]) for i in range(8)]
    return (y_p, y_s, P[0], S[0], P[1], S[1], P[2], S[2], P[3], S[3],
            P[4], S[4], P[5], S[5], P[6], S[6], P[7], S[7])
```

```python
import functools
import math

import jax
import jax.numpy as jnp
from jax import lax
from jax.experimental import pallas as pl
from jax.experimental.pallas import tpu as pltpu

F32 = jnp.float32
BF16 = jnp.bfloat16

GDN_HEADS = 4
GDN_DK = 128
GDN_DV = 128
GDN_CONV = 4
GDN_CHUNK = 128
SB_HEADS = 8
SB_DH = 64
FOX_HEADS = 8
FOX_DH = 64
FFN_CONV = 3
EPS = 1e-6
PAGE = 128

GDN_QK = GDN_HEADS * GDN_DK
GDN_V = GDN_HEADS * GDN_DV
GDN_CH = 2 * GDN_QK + GDN_V
SB_W = SB_HEADS * SB_DH
FOX_W = FOX_HEADS * FOX_DH

C_QKV = 0
C_Z = 1536
C_SB = 2048
C_FOX = 3584
C_GATE = 5120
SM_F, SM_A, SM_B = 0, 8, 12

LANES = 128
ROWPAD = 8
UNDERFLOW = -104.0
VMEM_LIMIT = 56 * 1024 * 1024


def _cparams(sem):
    return pltpu.CompilerParams(dimension_semantics=sem, vmem_limit_bytes=VMEM_LIMIT)


def _dot(a, b):
    return lax.dot_general(a, b, (((1,), (0,)), ((), ())), preferred_element_type=F32)


def _dot_nt(a, b):
    return lax.dot_general(a, b, (((1,), (1,)), ((), ())), preferred_element_type=F32)


def _dot_tn(a, b):
    return lax.dot_general(a, b, (((0,), (0,)), ((), ())), preferred_element_type=F32)


def _split2(a):
    hi = a.astype(BF16)
    lo = (a - hi.astype(F32)).astype(BF16)
    return hi, lo


def _split3(a):
    hi = a.astype(BF16)
    r = a - hi.astype(F32)
    mid = r.astype(BF16)
    lo = (r - mid.astype(F32)).astype(BF16)
    return hi, mid, lo


def _dot_exact_lhs(e, a):
    hi, mid, lo = _split3(a)
    return _dot(e, hi) + _dot(e, mid) + _dot(e, lo)


def _dot_exact_rhs(a, e):
    hi, mid, lo = _split3(a)
    return _dot(hi, e) + _dot(mid, e) + _dot(lo, e)


def _dot_hp(a, b):
    ah, al = _split2(a)
    bh, bl = _split2(b)
    return _dot(ah, bh) + _dot(ah, bl) + _dot(al, bh)


def _softplus(x):
    return jnp.maximum(x, 0.0) + jnp.log1p(jnp.exp(-jnp.abs(x)))


def _log_sigmoid(x):
    return jnp.minimum(x, 0.0) - jnp.log1p(jnp.exp(-jnp.abs(x)))


def _sigmoid(x):
    return 1.0 / (1.0 + jnp.exp(-x))


def _silu(x):
    return x * _sigmoid(x)


def _iota(shape, dim):
    return lax.broadcasted_iota(jnp.int32, shape, dim)


def _ada_kernel(c_ref, w_ref, b_ref, o_ref):
    c = c_ref[...]
    o_ref[...] = _dot(_silu(c).astype(BF16), w_ref[...].astype(BF16)) + b_ref[...]


def _ada(c_all, w_ada, b_ada):
    r, d = c_all.shape
    n = w_ada.shape[1]
    tn = d
    return pl.pallas_call(
        _ada_kernel,
        out_shape=jax.ShapeDtypeStruct((r, n), F32),
        grid=(n // tn,),
        in_specs=[pl.BlockSpec((r, d), lambda j: (0, 0)),
                  pl.BlockSpec((d, tn), lambda j: (0, j)),
                  pl.BlockSpec((1, tn), lambda j: (0, j))],
        out_specs=pl.BlockSpec((r, tn), lambda j: (0, j)),
        compiler_params=_cparams(("arbitrary",)),
        name="ada_mod",
    )(c_all, w_ada, b_ada.reshape(1, n))


def _mod_rows(ref, rows):
    v = ref[...]
    return v if v.shape[0] == rows else v[0:1]


def _norm_matmul_kernel(x_ref, g_ref, sc_ref, sh_ref, w_ref, o_ref, u_ref):
    @pl.when(pl.program_id(1) == 0)
    def _():
        x = x_ref[...]
        y = x * lax.rsqrt(jnp.mean(x * x, axis=-1, keepdims=True) + EPS) * g_ref[...]
        rows = x.shape[0]
        u_ref[...] = (y * (1.0 + _mod_rows(sc_ref, rows)) + _mod_rows(sh_ref, rows)).astype(BF16)

    o_ref[...] = _dot(u_ref[...], w_ref[...])


def _mod_spec(tm, rows_per_seq, d, ncols_grid):
    if rows_per_seq == ROWPAD:
        shape, imap = (tm, d), (lambda i: i)
    else:
        shape, imap = (ROWPAD, d), (lambda i: (i * tm) // rows_per_seq)
    if ncols_grid:
        return pl.BlockSpec(shape, lambda i, j: (imap(i), 0))
    return pl.BlockSpec(shape, lambda i: (imap(i), 0))


def _norm_matmul(x, g, sc, sh, w, rows_per_seq, tm, tn, name):
    t, d = x.shape
    n = w.shape[1]
    return pl.pallas_call(
        _norm_matmul_kernel,
        out_shape=jax.ShapeDtypeStruct((t, n), F32),
        grid=(t // tm, n // tn),
        in_specs=[pl.BlockSpec((tm, d), lambda i, j: (i, 0)),
                  pl.BlockSpec((1, d), lambda i, j: (0, 0)),
                  _mod_spec(tm, rows_per_seq, d, True),
                  _mod_spec(tm, rows_per_seq, d, True),
                  pl.BlockSpec((d, tn), lambda i, j: (0, j))],
        out_specs=pl.BlockSpec((tm, tn), lambda i, j: (i, j)),
        scratch_shapes=[pltpu.VMEM((tm, d), BF16)],
        compiler_params=_cparams(("parallel", "arbitrary")),
        name=name,
    )(x, g.reshape(1, d), sc, sh, w)


def _gdn_kernel(qkv_ref, sm_ref, z_ref, conv0_ref, s0_ref, cw_ref, hp_ref, ng_ref,
                o_ref, sout_ref, xbuf, s_ref, *, chunk, n_valid):
    @pl.when(pl.program_id(1) == 0)
    def _():
        xbuf[:, 0:ROWPAD, :] = conv0_ref[...]
        s_ref[...] = s0_ref[...]

    for s in range(qkv_ref.shape[0]):
        _gdn_chunk(qkv_ref.at[s], sm_ref.at[s], z_ref.at[s], cw_ref, hp_ref, ng_ref, o_ref.at[s],
                   xbuf.at[s], s_ref.at[s], chunk, n_valid)
    sout_ref[...] = s_ref[...]


def _gdn_chunk(qkv_ref, sm_ref, z_ref, cw_ref, hp_ref, ng_ref, o_ref, xbuf, s_ref, chunk, n_valid):
    c = chunk
    halo = GDN_CONV - 1
    xbuf[ROWPAD:ROWPAD + c, :] = qkv_ref[...]
    cw = cw_ref[...]
    y = xbuf[ROWPAD - halo:ROWPAD - halo + c, :] * cw[0:1]
    for i in range(1, GDN_CONV):
        y = y + xbuf[ROWPAD - halo + i:ROWPAD - halo + i + c, :] * cw[i:i + 1]
    xbuf[ROWPAD - halo:ROWPAD, :] = xbuf[ROWPAD - halo + c:ROWPAD + c, :]
    y = _silu(y)

    sm = sm_ref[...]
    hp = hp_ref[...]
    gmat = hp[0:1] * _softplus(sm + hp[1:2])
    beta = _sigmoid(sm)
    if n_valid < c:
        live = _iota((c, LANES), 0) < n_valid
        gmat = jnp.where(live, gmat, 0.0)
        beta = jnp.where(live, beta, 0.0)

    ri = _iota((c, c), 0)
    ci = _iota((c, c), 1)
    incl = ri >= ci
    tri = jnp.where(incl, 1.0, 0.0).astype(BF16)
    eye = jnp.where(ri == ci, 1.0, 0.0)
    gc_all = _dot_exact_lhs(tri, gmat)
    sel = jnp.where(_iota((ROWPAD, LANES), 1) == _iota((ROWPAD, LANES), 0) + SM_A, 1.0, 0.0).astype(BF16)
    gh, gm, gl = _split3(gc_all)
    gr_all = _dot_nt(sel, gh) + _dot_nt(sel, gm) + _dot_nt(sel, gl)
    nsteps = max(1, int(math.ceil(math.log2(c))))

    for h in range(GDN_HEADS):
        ls = slice(h * GDN_DK, (h + 1) * GDN_DK)
        qh = y[:, ls]
        kh = y[:, GDN_QK + h * GDN_DK:GDN_QK + (h + 1) * GDN_DK]
        vh = y[:, 2 * GDN_QK + h * GDN_DV:2 * GDN_QK + (h + 1) * GDN_DV]
        qh = qh * lax.rsqrt(jnp.sum(qh * qh, axis=-1, keepdims=True) + EPS) * (GDN_DK ** -0.5)
        kh = kh * lax.rsqrt(jnp.sum(kh * kh, axis=-1, keepdims=True) + EPS)
        b_col = beta[:, SM_B + h:SM_B + h + 1]
        gc = gc_all[:, SM_A + h:SM_A + h + 1]
        decay = jnp.where(incl, jnp.exp(jnp.where(incl, gc - gr_all[h:h + 1, :], 0.0)), 0.0)
        eg = jnp.exp(gc)
        kb = kh * b_col
        vb = vh * b_col
        kh_b = kh.astype(BF16)
        kkt = _dot_nt(kb.astype(BF16), kh_b) * decay
        p = -jnp.where(ri > ci, kkt, 0.0)
        tmat = eye + p
        for _ in range(nsteps - 1):
            p = _dot_hp(p, p)
            tmat = tmat + _dot_hp(tmat, p)
        t_b = tmat.astype(BF16)
        u = _dot(t_b, vb.astype(BF16))
        w = _dot(t_b, (kb * eg).astype(BF16))
        s_old = s_ref[h]
        s_b = s_old.astype(BF16)
        v_new = u - _dot(w.astype(BF16), s_b)
        v_new_b = v_new.astype(BF16)
        qk = _dot_nt(qh.astype(BF16), kh_b) * decay
        o = _dot((qh * eg).astype(BF16), s_b) + _dot(qk.astype(BF16), v_new_b)
        g_last = gc[c - 1:c, :]
        k_dec = kh * jnp.exp(g_last - gc)
        s_ref[h] = s_old * jnp.exp(g_last) + _dot_tn(k_dec.astype(BF16), v_new_b)
        on = o * lax.rsqrt(jnp.mean(o * o, axis=-1, keepdims=True) + EPS) * ng_ref[...]
        zh = z_ref[:, h * GDN_DV:(h + 1) * GDN_DV]
        o_ref[:, h * GDN_DV:(h + 1) * GDN_DV] = on * _silu(zh)


def _gdn(proj, conv0, s0, conv_w, a_log, dt_bias, norm_g, n_seq, rows_per_seq, chunk, n_valid):
    t, n_proj = proj.shape
    nc = rows_per_seq // chunk
    bg = next(g for g in ((2, 1) if nc > 1 else (4, 2, 1)) if n_seq % g == 0)
    p3 = proj.reshape(n_seq, rows_per_seq, n_proj)
    hp = jnp.zeros((ROWPAD, LANES), F32)
    hp = hp.at[0, SM_A:SM_A + GDN_HEADS].set(-jnp.exp(a_log))
    hp = hp.at[1, SM_A:SM_A + GDN_HEADS].set(dt_bias)
    kern = functools.partial(_gdn_kernel, chunk=chunk, n_valid=n_valid)
    state_spec = pl.BlockSpec((bg, GDN_HEADS, GDN_DK, GDN_DV), lambda g, c: (g, 0, 0, 0))
    go, s_new = pl.pallas_call(
        kern,
        out_shape=(jax.ShapeDtypeStruct((n_seq, rows_per_seq, GDN_V), F32),
                   jax.ShapeDtypeStruct((n_seq, GDN_HEADS, GDN_DK, GDN_DV), F32)),
        grid=(n_seq // bg, nc),
        in_specs=[pl.BlockSpec((bg, chunk, GDN_CH), lambda g, c: (g, c, C_QKV // GDN_CH)),
                  pl.BlockSpec((bg, chunk, LANES), lambda g, c: (g, c, n_proj // LANES - 1)),
                  pl.BlockSpec((bg, chunk, GDN_V), lambda g, c: (g, c, C_Z // GDN_V)),
                  pl.BlockSpec((bg, ROWPAD, GDN_CH), lambda g, c: (g, 0, 0)),
                  state_spec,
                  pl.BlockSpec((GDN_CONV, GDN_CH), lambda g, c: (0, 0)),
                  pl.BlockSpec((ROWPAD, LANES), lambda g, c: (0, 0)),
                  pl.BlockSpec((1, GDN_DV), lambda g, c: (0, 0))],
        out_specs=(pl.BlockSpec((bg, chunk, GDN_V), lambda g, c: (g, c, 0)), state_spec),
        scratch_shapes=[pltpu.VMEM((bg, ROWPAD + chunk, GDN_CH), F32),
                        pltpu.VMEM((bg, GDN_HEADS, GDN_DK, GDN_DV), F32)],
        compiler_params=_cparams(("arbitrary", "arbitrary")),
        name="gdn_scan",
    )(p3, p3, p3, conv0, s0, conv_w, hp, norm_g.reshape(1, GDN_DV))
    return go.reshape(t, GDN_V), s_new


def _fox_pre_kernel(sm_ref, b_ref, logf_ref, f_ref, ft_ref, carry):
    @pl.when(pl.program_id(1) == 0)
    def _():
        carry[...] = jnp.zeros_like(carry)

    tm = sm_ref.shape[0]
    lf = _log_sigmoid(sm_ref[...] + b_ref[...])
    logf_ref[...] = lf
    tri = jnp.where(_iota((tm, tm), 0) >= _iota((tm, tm), 1), 1.0, 0.0).astype(BF16)
    cs = _dot_exact_lhs(tri, lf) + carry[...]
    f_ref[...] = cs
    carry[...] = cs[tm - 1:tm, :]
    sel = jnp.where(_iota((ROWPAD, LANES), 0) == _iota((ROWPAD, LANES), 1), 1.0, 0.0).astype(BF16)
    hi, mid, lo = _split3(cs)
    ft_ref[0] = _dot_nt(sel, hi) + _dot_nt(sel, mid) + _dot_nt(sel, lo)


def _fox_pre(proj, f_bias, n_seq, rows_per_seq, tm):
    t = proj.shape[0]
    nb = rows_per_seq // tm
    brow = jnp.zeros((1, LANES), F32).at[0, SM_F:SM_F + FOX_HEADS].set(f_bias)
    return pl.pallas_call(
        _fox_pre_kernel,
        out_shape=(jax.ShapeDtypeStruct((t, LANES), F32),
                   jax.ShapeDtypeStruct((t, LANES), F32),
                   jax.ShapeDtypeStruct((n_seq, ROWPAD, rows_per_seq), F32)),
        grid=(n_seq, nb),
        in_specs=[pl.BlockSpec((tm, LANES), lambda b, i: (b * nb + i, proj.shape[1] // LANES - 1)),
                  pl.BlockSpec((1, LANES), lambda b, i: (0, 0))],
        out_specs=(pl.BlockSpec((tm, LANES), lambda b, i: (b * nb + i, 0)),
                   pl.BlockSpec((tm, LANES), lambda b, i: (b * nb + i, 0)),
                   pl.BlockSpec((1, ROWPAD, tm), lambda b, i: (b, 0, i))),
        scratch_shapes=[pltpu.VMEM((1, LANES), F32)],
        compiler_params=_cparams(("arbitrary", "arbitrary")),
        name="fox_pre",
    )(proj, brow)


def _head_mask(h, dh):
    return (_iota((1, LANES), 1) // dh) == h


def _sb_prompt_kernel(q_ref, k_ref, v_ref, u_ref, o_ref, *, tq, tk):
    i = pl.program_id(2)
    nd = tq // tk
    q2 = q_ref[...] * (SB_DH ** -0.5)
    masks = [_head_mask(h, SB_DH) for h in range(2)]
    qhs = [jnp.where(m, q2, 0.0).astype(BF16) for m in masks]
    ustrict = u_ref[...]

    def block(j, carries, acc, masked):
        start = pl.multiple_of(j * tk, tk)
        kb = k_ref[pl.ds(start, tk), :].astype(BF16)
        vfull = v_ref[pl.ds(start, tk), :]
        if masked:
            earlier = (j * tk + _iota((tq, tk), 1)) < (i * tq + _iota((tq, tk), 0))
        out_c = []
        for h in range(2):
            s = _dot_nt(qhs[h], kb)
            sp = _softplus(s)
            lf = -sp
            if masked:
                lf = jnp.where(earlier, lf, 0.0)
            hi, lo = _split2(lf)
            cum = _dot(hi, ustrict) + _dot(lo, ustrict)
            a = jnp.exp(s - sp + carries[h] + cum)
            if masked:
                a = jnp.where(earlier, a, 0.0)
            vb = jnp.where(masks[h], vfull, 0.0).astype(BF16)
            acc = acc + _dot(a.astype(BF16), vb)
            out_c.append(carries[h] + cum[:, 0:1] + lf[:, 0:1])
        return out_c, acc

    carries = [jnp.zeros((tq, 1), F32), jnp.zeros((tq, 1), F32)]
    acc = jnp.zeros((tq, LANES), F32)
    for d in range(nd):
        carries, acc = block(i * nd + (nd - 1 - d), carries, acc, True)

    def cond(st):
        j, c0, c1, _ = st
        return jnp.logical_and(j >= 0, jnp.maximum(jnp.max(c0), jnp.max(c1)) > UNDERFLOW)

    def body(st):
        j, c0, c1, a = st
        (c0, c1), a = block(j, [c0, c1], a, False)
        return j - 1, c0, c1, a

    st = lax.while_loop(cond, body, (i * nd - 1, carries[0], carries[1], acc))
    o_ref[...] = st[3]


def _sb_prompt(proj, n_seq, seq, tq, tk):
    t = proj.shape[0]
    nq = seq // tq
    cq, ck, cv = C_SB // LANES, (C_SB + SB_W) // LANES, (C_SB + 2 * SB_W) // LANES
    ustrict = (jnp.arange(tk)[:, None] > jnp.arange(tk)[None, :]).astype(BF16)
    return pl.pallas_call(
        functools.partial(_sb_prompt_kernel, tq=tq, tk=tk),
        out_shape=jax.ShapeDtypeStruct((t, SB_W), F32),
        grid=(n_seq, SB_HEADS // 2, nq),
        in_specs=[pl.BlockSpec((tq, LANES), lambda b, p, i: (b * nq + i, cq + p)),
                  pl.BlockSpec((seq, LANES), lambda b, p, i: (b, ck + p)),
                  pl.BlockSpec((seq, LANES), lambda b, p, i: (b, cv + p)),
                  pl.BlockSpec((tk, tk), lambda b, p, i: (0, 0))],
        out_specs=pl.BlockSpec((tq, LANES), lambda b, p, i: (b * nq + i, p)),
        compiler_params=_cparams(("parallel", "parallel", "arbitrary")),
        name="sb_prompt",
    )(proj, proj, proj, ustrict)


def _fox_prompt_kernel(q_ref, k_ref, v_ref, f_ref, ft_ref, o_ref, *, tq, tk):
    p = pl.program_id(1)
    i = pl.program_id(2)
    nd = tq // tk
    q2 = q_ref[...] * (FOX_DH ** -0.5)
    masks = [_head_mask(h, FOX_DH) for h in range(2)]
    qhs = [jnp.where(m, q2, 0.0).astype(BF16) for m in masks]
    fblk = f_ref[...]
    lane = _iota((1, LANES), 1)
    fq = [jnp.sum(jnp.where(lane == 2 * p + h, fblk, 0.0), axis=1, keepdims=True) for h in range(2)]

    def block(j, m, l, acc, masked):
        start = pl.multiple_of(j * tk, tk)
        kb = k_ref[pl.ds(start, tk), :].astype(BF16)
        vfull = v_ref[pl.ds(start, tk), :]
        if masked:
            visible = (j * tk + _iota((tq, tk), 1)) <= (i * tq + _iota((tq, tk), 0))
        ftb = ft_ref[0, :, pl.ds(start, tk)]
        hrow = _iota((ROWPAD, tk), 0)
        m2, l2, a2 = [], [], []
        for h in range(2):
            fk = jnp.sum(jnp.where(hrow == 2 * p + h, ftb, 0.0), axis=0, keepdims=True)
            s = _dot_nt(qhs[h], kb) + (fq[h] - fk)
            if masked:
                s = jnp.where(visible, s, -jnp.inf)
            m_new = jnp.maximum(m[h], jnp.max(s, axis=1, keepdims=True))
            alpha = jnp.exp(m[h] - m_new)
            pr = jnp.exp(s - m_new)
            vb = jnp.where(masks[h], vfull, 0.0).astype(BF16)
            l2.append(l[h] * alpha + jnp.sum(pr, axis=1, keepdims=True))
            a2.append(acc[h] * alpha + _dot(pr.astype(BF16), vb))
            m2.append(m_new)
        return m2, l2, a2

    def body(j, st):
        m0, m1, l0, l1, a0, a1 = st
        m, l, a = block(j, [m0, m1], [l0, l1], [a0, a1], False)
        return m[0], m[1], l[0], l[1], a[0], a[1]

    neg = jnp.full((tq, 1), -jnp.inf, F32)
    zc = jnp.zeros((tq, 1), F32)
    za = jnp.zeros((tq, LANES), F32)
    st = lax.fori_loop(0, i * nd, body, (neg, neg, zc, zc, za, za))
    m, l, a = [st[0], st[1]], [st[2], st[3]], [st[4], st[5]]
    for d in range(nd):
        m, l, a = block(i * nd + d, m, l, a, True)
    o_ref[...] = a[0] * (1.0 / l[0]) + a[1] * (1.0 / l[1])


def _fox_prompt(proj, fsum, fsum_t, n_seq, seq, tq, tk):
    t = proj.shape[0]
    nq = seq // tq
    cq, ck, cv = C_FOX // LANES, (C_FOX + FOX_W) // LANES, (C_FOX + 2 * FOX_W) // LANES
    return pl.pallas_call(
        functools.partial(_fox_prompt_kernel, tq=tq, tk=tk),
        out_shape=jax.ShapeDtypeStruct((t, FOX_W), F32),
        grid=(n_seq, FOX_HEADS // 2, nq),
        in_specs=[pl.BlockSpec((tq, LANES), lambda b, p, i: (b * nq + i, cq + p)),
                  pl.BlockSpec((seq, LANES), lambda b, p, i: (b, ck + p)),
                  pl.BlockSpec((seq, LANES), lambda b, p, i: (b, cv + p)),
                  pl.BlockSpec((tq, LANES), lambda b, p, i: (b * nq + i, 0)),
                  pl.BlockSpec((1, ROWPAD, seq), lambda b, p, i: (b, 0, 0))],
        out_specs=pl.BlockSpec((tq, LANES), lambda b, p, i: (b * nq + i, p)),
        compiler_params=_cparams(("parallel", "parallel", "arbitrary")),
        name="fox_prompt",
    )(proj, proj, proj, fsum, fsum_t)


def _block_diag_q(q, heads, dh):
    shape = (heads * ROWPAD, heads * dh)
    qt = jnp.concatenate([q] * heads, axis=0) * (dh ** -0.5)
    own = (_iota(shape, 0) // ROWPAD) == (_iota(shape, 1) // dh)
    return jnp.where(own, qt, 0.0).astype(BF16)


def _diag_rows(acc, heads, dh):
    own = (_iota(acc.shape, 0) // ROWPAD) == (_iota(acc.shape, 1) // dh)
    m = jnp.where(own, acc, 0.0)
    out = m[0:ROWPAD]
    for h in range(1, heads):
        out = out + m[h * ROWPAD:(h + 1) * ROWPAD]
    return out


def _lane_suffix(x):
    n = x.shape[-1]
    ax = len(x.shape) - 1
    lane = _iota(x.shape, ax)
    acc = x
    d = 1
    while d < n:
        shifted = pltpu.roll(acc, n - d, ax)
        acc = acc + jnp.where(lane < n - d, shifted, 0.0)
        d *= 2
    return acc - x, acc


def _expand_heads(x):
    return jnp.concatenate([jnp.broadcast_to(x[h:h + 1], (ROWPAD, x.shape[1])) for h in range(x.shape[0])],
                           axis=0)


def _page_bf16(page):
    return page.reshape(page.shape[0] * page.shape[1], page.shape[2]).astype(BF16)


def _sb_page(qbd, kpage, vpage, carry, visible):
    vf = _page_bf16(vpage)
    s = _dot(qbd, _page_bf16(kpage))
    sp = _softplus(s)
    lf = -sp
    if visible is not None:
        lf = jnp.where(visible, lf, 0.0)
    later, incl = _lane_suffix(lf)
    a = jnp.exp(s - sp + carry + later)
    if visible is not None:
        a = jnp.where(visible, a, 0.0)
    return _dot_nt(a.astype(BF16), vf), carry + incl[:, 0:1]


def _sb_decode_kernel(pt_ref, q_ref, kn_ref, vn_ref, k_hbm, v_hbm, o_ref, kbuf, vbuf, sem, *, layer, n_pages):
    b = pl.program_id(0)
    heads, dh = SB_HEADS, SB_DH
    rows = heads * ROWPAD
    qbd = _block_diag_q(q_ref[...], heads, dh)

    def copies(p, slot):
        page = pt_ref[b, p]
        return (pltpu.make_async_copy(k_hbm.at[layer, page], kbuf.at[slot], sem.at[0, slot]),
                pltpu.make_async_copy(v_hbm.at[layer, page], vbuf.at[slot], sem.at[1, slot]))

    def start(p, slot):
        for cp in copies(p, slot):
            cp.start()

    def wait(p, slot):
        for cp in copies(p, slot):
            cp.wait()

    start(n_pages - 1, 0)
    visible = _iota((rows, PAGE), 1) < (_iota((rows, PAGE), 0) % ROWPAD)
    acc, carry = _sb_page(qbd, kn_ref[0], vn_ref[0], jnp.zeros((rows, 1), F32), visible)

    def cond(st):
        p, cr, _ = st
        return jnp.logical_and(p >= 0, jnp.max(cr) > UNDERFLOW)

    def body(st):
        p, cr, ac = st
        slot = (n_pages - 1 - p) % 2
        wait(p, slot)

        @pl.when(p > 0)
        def _():
            start(p - 1, 1 - slot)

        d, cr = _sb_page(qbd, kbuf[slot], vbuf[slot], cr, None)
        return p - 1, cr, ac + d

    p_exit, carry, acc = lax.while_loop(cond, body, (n_pages - 1, carry, acc))

    @pl.when(p_exit >= 0)
    def _():
        wait(p_exit, (n_pages - 1 - p_exit) % 2)

    o_ref[...] = _diag_rows(acc, heads, dh)


def _sb_decode(page_table, proj, kn, vn, cache_k, cache_v, layer):
    n_seq, n_pages = page_table.shape
    heads, dh = SB_HEADS, SB_DH
    page_shape = (heads, dh, PAGE)
    new_spec = pl.BlockSpec((1,) + page_shape, lambda b, pt: (b, 0, 0, 0))
    grid_spec = pltpu.PrefetchScalarGridSpec(
        num_scalar_prefetch=1,
        grid=(n_seq,),
        in_specs=[pl.BlockSpec((ROWPAD, SB_W), lambda b, pt: (b, C_SB // SB_W)),
                  new_spec, new_spec,
                  pl.BlockSpec(memory_space=pl.ANY), pl.BlockSpec(memory_space=pl.ANY)],
        out_specs=pl.BlockSpec((ROWPAD, SB_W), lambda b, pt: (b, 0)),
        scratch_shapes=[pltpu.VMEM((2,) + page_shape, F32), pltpu.VMEM((2,) + page_shape, F32),
                        pltpu.SemaphoreType.DMA((2, 2))],
    )
    return pl.pallas_call(
        functools.partial(_sb_decode_kernel, layer=layer, n_pages=n_pages),
        out_shape=jax.ShapeDtypeStruct((n_seq * ROWPAD, SB_W), F32),
        grid_spec=grid_spec,
        compiler_params=_cparams(("arbitrary",)),
        name="sb_decode",
    )(page_table, proj, kn, vn, cache_k, cache_v)


def _page_suffix_kernel(x_ref, later_ref, incl_ref):
    x = x_ref[...]
    n = x.shape[1]
    ustrict = jnp.where(_iota((n, n), 0) > _iota((n, n), 1), 1.0, 0.0).astype(BF16)
    later = _dot_exact_rhs(x, ustrict)
    later_ref[...] = later
    incl_ref[...] = later + x


def _page_suffix(lf_pages):
    shape = lf_pages.shape
    rows = math.prod(shape[:-1])
    tm = _tile(rows, 2048)
    spec = pl.BlockSpec((tm, shape[-1]), lambda i: (i, 0))
    later, incl = pl.pallas_call(
        _page_suffix_kernel,
        out_shape=(jax.ShapeDtypeStruct((rows, shape[-1]), F32),) * 2,
        grid=(rows // tm,),
        in_specs=[spec],
        out_specs=(spec, spec),
        compiler_params=_cparams(("parallel",)),
        name="page_suffix",
    )(lf_pages.reshape(rows, shape[-1]))
    return later.reshape(shape), incl.reshape(shape)


def _fox_decode_kernel(pt_ref, q_ref, kn_ref, vn_ref, pc_ref, pn_ref, *refs, group):
    k_refs, v_refs = refs[0:group], refs[group:2 * group]
    lt_refs, in_refs = refs[2 * group:3 * group], refs[3 * group:4 * group]
    o_ref, m_ref, l_ref, acc, dsum = refs[4 * group:]
    j = pl.program_id(1)
    heads, dh = FOX_HEADS, FOX_DH
    rows = heads * ROWPAD
    qbd = _block_diag_q(q_ref[...], heads, dh)
    pcol = pc_ref[0]

    @pl.when(j == 0)
    def _():
        visible = _iota((rows, PAGE), 1) <= (_iota((rows, PAGE), 0) % ROWPAD)
        s = _dot(qbd, _page_bf16(kn_ref[0])) + (pcol - _expand_heads(pn_ref[0]))
        s = jnp.where(visible, s, -jnp.inf)
        m = jnp.max(s, axis=1, keepdims=True)
        pr = jnp.exp(s - m)
        m_ref[...] = m
        l_ref[...] = jnp.sum(pr, axis=1, keepdims=True)
        acc[...] = _dot_nt(pr.astype(BF16), _page_bf16(vn_ref[0]))
        dsum[...] = jnp.zeros_like(dsum)

    @pl.when(j > 0)
    def _():
        run = dsum[...]
        parts = []
        for g in range(group):
            bias = pcol + _expand_heads(run + lt_refs[g][0, 0])
            parts.append(_dot(qbd, _page_bf16(k_refs[g][0, 0])) + bias)
            run = run + in_refs[g][0, 0][:, 0:1]
        dsum[...] = run
        s = jnp.concatenate(parts, axis=1)
        m_old = m_ref[...]
        m_new = jnp.maximum(m_old, jnp.max(s, axis=1, keepdims=True))
        alpha = jnp.exp(m_old - m_new)
        pr = jnp.exp(s - m_new)
        l_ref[...] = l_ref[...] * alpha + jnp.sum(pr, axis=1, keepdims=True)
        pv = _dot_nt(pr[:, 0:PAGE].astype(BF16), _page_bf16(v_refs[0][0, 0]))
        for g in range(1, group):
            pv = pv + _dot_nt(pr[:, g * PAGE:(g + 1) * PAGE].astype(BF16), _page_bf16(v_refs[g][0, 0]))
        acc[...] = acc[...] * alpha + pv
        m_ref[...] = m_new

    o_ref[...] = _diag_rows(acc[...] * (1.0 / l_ref[...]), heads, dh)


def _fox_decode(page_table, proj, kn, vn, pcol, pnew, cache_k, cache_v, lf_later, lf_incl, layer):
    n_seq, n_pages = page_table.shape
    heads, dh = FOX_HEADS, FOX_DH
    rows = heads * ROWPAD
    page_shape = (heads, dh, PAGE)
    group = next(g for g in (4, 2, 1) if n_pages % g == 0)

    def page_of(g):
        return lambda b, j, pt: pt[b, n_pages - 1 - (jnp.maximum(j, 1) - 1) * group - g]

    def kv_spec(g):
        return pl.BlockSpec((1, 1) + page_shape, lambda b, j, pt: (layer, page_of(g)(b, j, pt), 0, 0, 0))

    def lf_spec(g):
        return pl.BlockSpec((1, 1, heads, PAGE), lambda b, j, pt: (layer, page_of(g)(b, j, pt), 0, 0))

    new_spec = pl.BlockSpec((1,) + page_shape, lambda b, j, pt: (b, 0, 0, 0))
    gs = range(group)
    grid_spec = pltpu.PrefetchScalarGridSpec(
        num_scalar_prefetch=1,
        grid=(n_seq, n_pages // group + 1),
        in_specs=[pl.BlockSpec((ROWPAD, FOX_W), lambda b, j, pt: (b, C_FOX // FOX_W)),
                  new_spec, new_spec,
                  pl.BlockSpec((1, rows, 1), lambda b, j, pt: (b, 0, 0)),
                  pl.BlockSpec((1, heads, PAGE), lambda b, j, pt: (b, 0, 0))]
                 + [kv_spec(g) for g in gs] + [kv_spec(g) for g in gs]
                 + [lf_spec(g) for g in gs] + [lf_spec(g) for g in gs],
        out_specs=pl.BlockSpec((ROWPAD, FOX_W), lambda b, j, pt: (b, 0)),
        scratch_shapes=[pltpu.VMEM((rows, 1), F32), pltpu.VMEM((rows, 1), F32),
                        pltpu.VMEM((rows, FOX_W), F32), pltpu.VMEM((heads, PAGE), F32)],
    )
    return pl.pallas_call(
        functools.partial(_fox_decode_kernel, group=group),
        out_shape=jax.ShapeDtypeStruct((n_seq * ROWPAD, FOX_W), F32),
        grid_spec=grid_spec,
        compiler_params=_cparams(("arbitrary", "arbitrary")),
        name="fox_decode",
    )(page_table, proj, kn, vn, pcol, pnew, *([cache_k] * group), *([cache_v] * group),
      *([lf_later] * group), *([lf_incl] * group))


def _merge_kernel(x_ref, go_ref, sb_ref, fx_ref, g0_ref, g1_ref, g2_ref, gt_ref,
                  wg_ref, ws_ref, wf_ref, wo_ref, o_ref):
    rows = x_ref.shape[0]
    a = _dot(go_ref[...].astype(BF16), wg_ref[...])
    b = _dot(sb_ref[...].astype(BF16), ws_ref[...])
    c = _dot(fx_ref[...].astype(BF16), wf_ref[...])
    merged = _sigmoid(g0_ref[...]) * a + _sigmoid(g1_ref[...]) * b + _sigmoid(g2_ref[...]) * c
    o_ref[...] = x_ref[...] + _mod_rows(gt_ref, rows) * _dot(merged.astype(BF16), wo_ref[...])


def _merge(x, go, sbo, foxo, proj, gt, wg, ws, wf, wo, rows_per_seq, tm):
    t, d = x.shape
    cg = C_GATE // d
    row = lambda w: pl.BlockSpec((tm, w), lambda i: (i, 0))
    full = lambda a: pl.BlockSpec(a.shape, lambda i: (0, 0))
    return pl.pallas_call(
        _merge_kernel,
        out_shape=jax.ShapeDtypeStruct((t, d), F32),
        grid=(t // tm,),
        in_specs=[row(d), row(GDN_V), row(SB_W), row(FOX_W),
                  pl.BlockSpec((tm, d), lambda i: (i, cg)),
                  pl.BlockSpec((tm, d), lambda i: (i, cg + 1)),
                  pl.BlockSpec((tm, d), lambda i: (i, cg + 2)),
                  _mod_spec(tm, rows_per_seq, d, False),
                  full(wg), full(ws), full(wf), full(wo)],
        out_specs=row(d),
        compiler_params=_cparams(("parallel",)),
        name="merge_out",
    )(x, go, sbo, foxo, proj, proj, proj, gt, wg, ws, wf, wo)


def _ffn_down_kernel(up_ref, prev_ref, x_ref, gt_ref, cw_ref, cb_ref, wd_ref, o_ref, xs, *, cwid):
    tm = up_ref.shape[0]
    d_ff = wd_ref.shape[0]
    halo = FFN_CONV - 1
    xs[0:ROWPAD, :] = prev_ref[0]
    xs[ROWPAD:ROWPAD + tm, :] = up_ref[...]

    def conv(c0):
        cs = slice(c0, c0 + cwid)
        y = cb_ref[:, cs] + xs[ROWPAD - halo:ROWPAD - halo + tm, cs] * cw_ref[0:1, cs]
        for i in range(1, FFN_CONV):
            y = y + xs[ROWPAD - halo + i:ROWPAD - halo + i + tm, cs] * cw_ref[i:i + 1, cs]
        return y

    acc = jnp.zeros((tm, o_ref.shape[1]), F32)
    for c0 in range(0, d_ff, cwid):
        hid = _silu(conv(c0)) * conv(d_ff + c0)
        acc = acc + _dot(hid.astype(BF16), wd_ref[c0:c0 + cwid, :])
    o_ref[...] = x_ref[...] + _mod_rows(gt_ref, tm) * acc


def _ffn_down(up, prev, x, gt, conv_w, conv_b, wd, rows_per_seq, tm):
    t, d = x.shape
    n2 = up.shape[1]
    d_ff = n2 // 2
    cwid = 256 if d_ff % 256 == 0 else LANES
    if rows_per_seq == ROWPAD:
        gt_spec = pl.BlockSpec((tm, d), lambda i: (i, 0))
    else:
        gt_spec = pl.BlockSpec((ROWPAD, d), lambda i: ((i * tm) // rows_per_seq, 0))
    return pl.pallas_call(
        functools.partial(_ffn_down_kernel, cwid=cwid),
        out_shape=jax.ShapeDtypeStruct((t, d), F32),
        grid=(t // tm,),
        in_specs=[pl.BlockSpec((tm, n2), lambda i: (i, 0)),
                  pl.BlockSpec((1, ROWPAD, n2), lambda i: (i, 0, 0)),
                  pl.BlockSpec((tm, d), lambda i: (i, 0)),
                  gt_spec,
                  pl.BlockSpec((FFN_CONV, n2), lambda i: (0, 0)),
                  pl.BlockSpec((1, n2), lambda i: (0, 0)),
                  pl.BlockSpec((d_ff, d), lambda i: (0, 0))],
        out_specs=pl.BlockSpec((tm, d), lambda i: (i, 0)),
        scratch_shapes=[pltpu.VMEM((ROWPAD + tm, n2), F32)],
        compiler_params=_cparams(("parallel",)),
        name="ffn_down",
    )(up, prev, x, gt, conv_w, conv_b.reshape(1, n2), wd)


def _final_norm_kernel(x_ref, g_ref, o_ref):
    x = x_ref[...]
    o_ref[...] = x * lax.rsqrt(jnp.mean(x * x, axis=-1, keepdims=True) + EPS) * g_ref[...]


def _final_norm(x, g, tm):
    t, d = x.shape
    return pl.pallas_call(
        _final_norm_kernel,
        out_shape=jax.ShapeDtypeStruct((t, d), F32),
        grid=(t // tm,),
        in_specs=[pl.BlockSpec((tm, d), lambda i: (i, 0)), pl.BlockSpec((1, d), lambda i: (0, 0))],
        out_specs=pl.BlockSpec((tm, d), lambda i: (i, 0)),
        compiler_params=_cparams(("parallel",)),
        name="final_norm",
    )(x, g.reshape(1, d))


def _tile(n, pref):
    t = min(pref, n)
    while n % t:
        t //= 2
    return t


def _prep_w_in(w_in, d):
    o_a = GDN_CH
    o_b = o_a + GDN_HEADS
    o_z = o_b + GDN_HEADS
    o_sb = o_z + GDN_V
    o_fox = o_sb + 3 * SB_W
    o_f = o_fox + 3 * FOX_W
    o_gate = o_f + FOX_HEADS
    small = jnp.concatenate([w_in[:, o_f:o_gate], w_in[:, o_a:o_b], w_in[:, o_b:o_z]], axis=1)
    small = jnp.pad(small, ((0, 0), (0, LANES - small.shape[1])))
    w = jnp.concatenate([w_in[:, :o_a], w_in[:, o_z:o_sb], w_in[:, o_sb:o_fox], w_in[:, o_fox:o_f],
                         w_in[:, o_gate:], small], axis=1)
    return w.astype(BF16)


def _rep(v):
    return jnp.repeat(v, ROWPAD, axis=0)


def _layer_group(x, mod, lw, rows_per_seq, n_valid, conv0, s0, ffn_prev_fn, attn_fn):
    t, d = x.shape
    n_seq = t // rows_per_seq
    sh1, sc1, gt1, sh2, sc2, gt2 = [_rep(m) for m in jnp.split(mod, 6, axis=-1)]
    tm = _tile(t, 512)
    n_proj = lw["w_in"].shape[1]
    proj = _norm_matmul(x, lw["norm1_g"], sc1, sh1, lw["w_in"], rows_per_seq, tm,
                        n_proj // 5, "proj_in")
    chunk = GDN_CHUNK if rows_per_seq % GDN_CHUNK == 0 else rows_per_seq
    go, s_new = _gdn(proj, conv0, s0, lw["gdn_conv_w"], lw["gdn_a_log"], lw["gdn_dt_bias"],
                     lw["gdn_norm_g"], n_seq, rows_per_seq, chunk, min(n_valid, chunk))
    logf, fsum, fsum_t = _fox_pre(proj, lw["fox_f_bias"], n_seq, rows_per_seq, _tile(rows_per_seq, 256))
    sbo, foxo = attn_fn(proj, fsum, fsum_t)
    x1 = _merge(x, go, sbo, foxo, proj, gt1, lw["w_br_gdn"], lw["w_br_sb"], lw["w_br_fox"], lw["w_out"],
                rows_per_seq, _tile(t, 256))
    n_up = lw["w_up"].shape[1]
    up = _norm_matmul(x1, lw["norm2_g"], sc2, sh2, lw["w_up"], rows_per_seq, tm, n_up // 4, "ffn_up")
    tm_d = ROWPAD if rows_per_seq == ROWPAD else _tile(rows_per_seq, 256)
    x2 = _ffn_down(up, ffn_prev_fn(up, tm_d), x1, gt2, lw["ffn_conv_w"], lw["ffn_conv_b"], lw["w_down"],
                   rows_per_seq, tm_d)
    return x2, proj, up, logf, s_new


def kernel(x_prompt, x_sample, cache_sb_k, cache_sb_v, cache_fox_k, cache_fox_v, cache_fox_logf, state_gdn, state_gdn_conv, state_ffn_conv, page_table, c_prompt, c_sample, norm1_g, norm2_g, final_g, w_ada, b_ada, w_in, gdn_conv_w, gdn_a_log, gdn_dt_bias, gdn_norm_g, fox_f_bias, w_br_gdn, w_br_sb, w_br_fox, w_out, w_up, ffn_conv_w, ffn_conv_b, w_down):
    bp, seq, d = x_prompt.shape
    bs, dseq, _ = x_sample.shape
    depth = w_in.shape[0]
    n2 = w_up.shape[2]
    assert dseq <= ROWPAD and seq % GDN_CHUNK == 0 and dseq >= GDN_CONV - 1

    xp = x_prompt.reshape(bp * seq, d)
    xs = jnp.pad(x_sample, ((0, 0), (0, ROWPAD - dseq), (0, 0))).reshape(bs * ROWPAD, d)
    n_c = bp + bs
    c_all = jnp.pad(jnp.concatenate([c_prompt, c_sample], axis=0), ((0, (-n_c) % ROWPAD), (0, 0)))
    sbk_pages, sbv_pages, fxk_pages, fxv_pages = [
        jnp.transpose(a, (0, 1, 3, 4, 2)) for a in (cache_sb_k, cache_sb_v, cache_fox_k, cache_fox_v)]
    lf_later, lf_incl = _page_suffix(jnp.transpose(cache_fox_logf, (0, 1, 3, 2)))

    def new_page(a, heads, dh):
        a = a.reshape(bs, ROWPAD, heads, dh).transpose(0, 2, 3, 1)
        return jnp.pad(a, ((0, 0), (0, 0), (0, 0), (0, PAGE - ROWPAD)))

    outs_p, outs_s = [], []
    for l in range(depth):
        lw = {"norm1_g": norm1_g[l], "norm2_g": norm2_g[l], "w_in": _prep_w_in(w_in[l], d),
              "gdn_conv_w": gdn_conv_w[l], "gdn_a_log": gdn_a_log[l], "gdn_dt_bias": gdn_dt_bias[l],
              "gdn_norm_g": gdn_norm_g[l], "fox_f_bias": fox_f_bias[l],
              "w_br_gdn": w_br_gdn[l].astype(BF16), "w_br_sb": w_br_sb[l].astype(BF16),
              "w_br_fox": w_br_fox[l].astype(BF16), "w_out": w_out[l].astype(BF16),
              "w_up": w_up[l].astype(BF16), "ffn_conv_w": ffn_conv_w[l], "ffn_conv_b": ffn_conv_b[l],
              "w_down": w_down[l].astype(BF16)}
        mod = _ada(c_all, w_ada[l], b_ada[l])

        def prompt_attn(proj, fsum, fsum_t):
            tq = _tile(seq, 256)
            tk = _tile(tq, 128)
            tf = _tile(seq, 512)
            return (_sb_prompt(proj, bp, seq, tq, tk), _fox_prompt(proj, fsum, fsum_t, bp, seq, tf, tf))

        def prompt_prev(up, tm_d):
            nt = up.shape[0] // tm_d
            tails = up.reshape(nt, tm_d, n2)[:, tm_d - ROWPAD:, :]
            prev = jnp.concatenate([jnp.zeros((1, ROWPAD, n2), F32), tails[:-1]], axis=0)
            first = (jnp.arange(nt) * tm_d) % seq == 0
            return jnp.where(first[:, None, None], 0.0, prev)

        xp, proj, up, logf, s_new = _layer_group(
            xp, mod[:bp], lw, seq, seq,
            jnp.zeros((bp, ROWPAD, GDN_CH), F32), jnp.zeros((bp, GDN_HEADS, GDN_DK, GDN_DV), F32),
            prompt_prev, prompt_attn)
        p3 = proj.reshape(bp, seq, -1)
        outs_p.append((
            p3[:, :, C_SB + SB_W:C_SB + 2 * SB_W].reshape(bp, seq, SB_HEADS, SB_DH),
            p3[:, :, C_SB + 2 * SB_W:C_SB + 3 * SB_W].reshape(bp, seq, SB_HEADS, SB_DH),
            p3[:, :, C_FOX + FOX_W:C_FOX + 2 * FOX_W].reshape(bp, seq, FOX_HEADS, FOX_DH),
            p3[:, :, C_FOX + 2 * FOX_W:C_FOX + 3 * FOX_W].reshape(bp, seq, FOX_HEADS, FOX_DH),
            logf.reshape(bp, seq, LANES)[:, :, SM_F:SM_F + FOX_HEADS],
            s_new,
            p3[:, seq - (GDN_CONV - 1):, C_QKV:C_QKV + GDN_CH],
            up.reshape(bp, seq, n2)[:, seq - (FFN_CONV - 1):, :]))

        def sample_attn(proj, fsum, fsum_t, l=l):
            def cols(c0, w):
                return proj[:, c0:c0 + w]
            sbo = _sb_decode(page_table, proj,
                             new_page(cols(C_SB + SB_W, SB_W), SB_HEADS, SB_DH),
                             new_page(cols(C_SB + 2 * SB_W, SB_W), SB_HEADS, SB_DH),
                             sbk_pages, sbv_pages, l)
            pcol = fsum_t.reshape(bs, FOX_HEADS * ROWPAD, 1)
            pnew = jnp.pad(fsum_t, ((0, 0), (0, 0), (0, PAGE - ROWPAD)))
            foxo = _fox_decode(page_table, proj,
                               new_page(cols(C_FOX + FOX_W, FOX_W), FOX_HEADS, FOX_DH),
                               new_page(cols(C_FOX + 2 * FOX_W, FOX_W), FOX_HEADS, FOX_DH),
                               pcol, pnew, fxk_pages, fxv_pages, lf_later, lf_incl, l)
            return sbo, foxo

        def sample_prev(up, tm_d, l=l):
            return jnp.pad(state_ffn_conv[l], ((0, 0), (ROWPAD - (FFN_CONV - 1), 0), (0, 0)))

        conv0 = jnp.pad(state_gdn_conv[l], ((0, 0), (ROWPAD - (GDN_CONV - 1), 0), (0, 0)))
        xs, proj, up, logf, s_new = _layer_group(
            xs, mod[bp:bp + bs], lw, ROWPAD, dseq, conv0, state_gdn[l], sample_prev, sample_attn)
        p3 = proj.reshape(bs, ROWPAD, -1)
        outs_s.append((
            p3[:, :dseq, C_SB + SB_W:C_SB + 2 * SB_W].reshape(bs, dseq, SB_HEADS, SB_DH),
            p3[:, :dseq, C_SB + 2 * SB_W:C_SB + 3 * SB_W].reshape(bs, dseq, SB_HEADS, SB_DH),
            p3[:, :dseq, C_FOX + FOX_W:C_FOX + 2 * FOX_W].reshape(bs, dseq, FOX_HEADS, FOX_DH),
            p3[:, :dseq, C_FOX + 2 * FOX_W:C_FOX + 3 * FOX_W].reshape(bs, dseq, FOX_HEADS, FOX_DH),
            logf.reshape(bs, ROWPAD, LANES)[:, :dseq, SM_F:SM_F + FOX_HEADS],
            s_new,
            p3[:, dseq - (GDN_CONV - 1):dseq, C_QKV:C_QKV + GDN_CH],
            up.reshape(bs, ROWPAD, n2)[:, dseq - (FFN_CONV - 1):dseq, :]))

    y_p = _final_norm(xp, final_g, _tile(xp.shape[0], 512)).reshape(bp, seq, d)
    y_s = _final_norm(xs, final_g, _tile(xs.shape[0], 512)).reshape(bs, ROWPAD, d)[:, :dseq]
    P = [jnp.stack([st[i] for st in outs_p]) for i in range(8)]
    S = [jnp.stack([st[i] for st in outs_s]) for i in range(8)]
    return (y_p, y_s, P[0], S[0], P[1], S[1], P[2], S[2], P[3], S[3],
            P[4], S[4], P[5], S[5], P[6], S[6], P[7], S[7])
```

```python
import functools
import math

import jax
import jax.numpy as jnp
from jax import lax
from jax.experimental import pallas as pl
from jax.experimental.pallas import tpu as pltpu

F32 = jnp.float32
BF16 = jnp.bfloat16

GDN_HEADS = 4
GDN_DK = 128
GDN_DV = 128
GDN_CONV = 4
GDN_CHUNK = 128
SB_HEADS = 8
SB_DH = 64
FOX_HEADS = 8
FOX_DH = 64
FFN_CONV = 3
EPS = 1e-6
PAGE = 128

GDN_QK = GDN_HEADS * GDN_DK
GDN_V = GDN_HEADS * GDN_DV
GDN_CH = 2 * GDN_QK + GDN_V
SB_W = SB_HEADS * SB_DH
FOX_W = FOX_HEADS * FOX_DH

C_QKV = 0
C_Z = 1536
C_SB = 2048
C_FOX = 3584
C_GATE = 5120
SM_F, SM_A, SM_B = 0, 8, 12

LANES = 128
ROWPAD = 8
UNDERFLOW = -104.0
SB_EAGER_BLOCKS = 2
VMEM_LIMIT = 56 * 1024 * 1024


def _cparams(sem):
    return pltpu.CompilerParams(dimension_semantics=sem, vmem_limit_bytes=VMEM_LIMIT)


def _dot(a, b):
    return lax.dot_general(a, b, (((1,), (0,)), ((), ())), preferred_element_type=F32)


def _dot_nt(a, b):
    return lax.dot_general(a, b, (((1,), (1,)), ((), ())), preferred_element_type=F32)


def _dot_tn(a, b):
    return lax.dot_general(a, b, (((0,), (0,)), ((), ())), preferred_element_type=F32)


def _split2(a):
    hi = a.astype(BF16)
    lo = (a - hi.astype(F32)).astype(BF16)
    return hi, lo


def _split3(a):
    hi = a.astype(BF16)
    r = a - hi.astype(F32)
    mid = r.astype(BF16)
    lo = (r - mid.astype(F32)).astype(BF16)
    return hi, mid, lo


def _dot_pieces(lhs, rhs):
    if all(a.shape[1] % LANES == 0 for a in lhs):
        return _dot(jnp.concatenate(lhs, axis=1), jnp.concatenate(rhs, axis=0))
    out = _dot(lhs[0], rhs[0])
    for a, b in zip(lhs[1:], rhs[1:]):
        out = out + _dot(a, b)
    return out


def _dot_exact_lhs(e, a):
    return _dot_pieces([e, e, e], list(_split3(a)))


def _dot_exact_rhs(a, e):
    return _dot_pieces(list(_split3(a)), [e, e, e])


def _dot_nt_exact_lhs(e, a):
    return _dot_nt(jnp.concatenate([e, e, e], axis=1), jnp.concatenate(_split3(a), axis=1))


def _dot_hp(a, b):
    ah, al = _split2(a)
    bh, bl = _split2(b)
    return _dot_pieces([ah, ah, al], [bh, bl, bh])


def _softplus(x):
    return jnp.maximum(x, 0.0) + jnp.log1p(jnp.exp(-jnp.abs(x)))


def _log_sigmoid(x):
    return jnp.minimum(x, 0.0) - jnp.log1p(jnp.exp(-jnp.abs(x)))


def _sigmoid(x):
    return 1.0 / (1.0 + jnp.exp(-x))


def _silu(x):
    return x * _sigmoid(x)


def _iota(shape, dim):
    return lax.broadcasted_iota(jnp.int32, shape, dim)


def _ada_kernel(c_ref, w_ref, b_ref, o_ref):
    c = c_ref[...]
    o_ref[...] = _dot(_silu(c).astype(BF16), w_ref[...].astype(BF16)) + b_ref[...]


def _ada(c_all, w_ada, b_ada):
    r, d = c_all.shape
    n = w_ada.shape[1]
    tn = d
    return pl.pallas_call(
        _ada_kernel,
        out_shape=jax.ShapeDtypeStruct((r, n), F32),
        grid=(n // tn,),
        in_specs=[pl.BlockSpec((r, d), lambda j: (0, 0)),
                  pl.BlockSpec((d, tn), lambda j: (0, j)),
                  pl.BlockSpec((1, tn), lambda j: (0, j))],
        out_specs=pl.BlockSpec((r, tn), lambda j: (0, j)),
        compiler_params=_cparams(("arbitrary",)),
        name="ada_mod",
    )(c_all, w_ada, b_ada.reshape(1, n))


def _mod_rows(ref, rows):
    v = ref[...]
    return v if v.shape[0] == rows else v[0:1]


def _norm_matmul_kernel(x_ref, g_ref, sc_ref, sh_ref, w_ref, o_ref, u_ref):
    @pl.when(pl.program_id(1) == 0)
    def _():
        x = x_ref[...]
        y = x * lax.rsqrt(jnp.mean(x * x, axis=-1, keepdims=True) + EPS) * g_ref[...]
        rows = x.shape[0]
        u_ref[...] = (y * (1.0 + _mod_rows(sc_ref, rows)) + _mod_rows(sh_ref, rows)).astype(BF16)

    o_ref[...] = _dot(u_ref[...], w_ref[...])


def _mod_spec(tm, rows_per_seq, d, ncols_grid):
    if rows_per_seq == ROWPAD:
        shape, imap = (tm, d), (lambda i: i)
    else:
        shape, imap = (ROWPAD, d), (lambda i: (i * tm) // rows_per_seq)
    if ncols_grid:
        return pl.BlockSpec(shape, lambda i, j: (imap(i), 0))
    return pl.BlockSpec(shape, lambda i: (imap(i), 0))


def _norm_matmul(x, g, sc, sh, w, rows_per_seq, tm, tn, name):
    t, d = x.shape
    n = w.shape[1]
    return pl.pallas_call(
        _norm_matmul_kernel,
        out_shape=jax.ShapeDtypeStruct((t, n), F32),
        grid=(t // tm, n // tn),
        in_specs=[pl.BlockSpec((tm, d), lambda i, j: (i, 0)),
                  pl.BlockSpec((1, d), lambda i, j: (0, 0)),
                  _mod_spec(tm, rows_per_seq, d, True),
                  _mod_spec(tm, rows_per_seq, d, True),
                  pl.BlockSpec((d, tn), lambda i, j: (0, j))],
        out_specs=pl.BlockSpec((tm, tn), lambda i, j: (i, j)),
        scratch_shapes=[pltpu.VMEM((tm, d), BF16)],
        compiler_params=_cparams(("parallel", "arbitrary")),
        name=name,
    )(x, g.reshape(1, d), sc, sh, w)


def _gdn_kernel(qkv_ref, sm_ref, z_ref, conv0_ref, s0_ref, cw_ref, hp_ref, ng_ref,
                o_ref, sout_ref, xbuf, s_ref, *, chunk, n_valid):
    @pl.when(pl.program_id(1) == 0)
    def _():
        xbuf[:, 0:ROWPAD, :] = conv0_ref[...]
        s_ref[...] = s0_ref[...]

    c = chunk
    halo = GDN_CONV - 1
    n_seq = qkv_ref.shape[0]
    chains = [(s, h) for s in range(n_seq) for h in range(GDN_HEADS)]
    ri = _iota((c, c), 0)
    ci = _iota((c, c), 1)
    incl = ri >= ci
    tri = jnp.where(incl, 1.0, 0.0).astype(BF16)
    eye = jnp.where(ri == ci, 1.0, 0.0)
    sel = jnp.where(_iota((ROWPAD, LANES), 1) == _iota((ROWPAD, LANES), 0) + SM_A, 1.0, 0.0).astype(BF16)
    cw = cw_ref[...]
    hp = hp_ref[...]

    ys, gmats, betas = [], [], []
    for s in range(n_seq):
        xb = xbuf.at[s]
        xb[ROWPAD:ROWPAD + c, :] = qkv_ref[s]
        y = xb[ROWPAD - halo:ROWPAD - halo + c, :] * cw[0:1]
        for i in range(1, GDN_CONV):
            y = y + xb[ROWPAD - halo + i:ROWPAD - halo + i + c, :] * cw[i:i + 1]
        xb[ROWPAD - halo:ROWPAD, :] = xb[ROWPAD - halo + c:ROWPAD + c, :]
        ys.append(_silu(y))
        sm = sm_ref[s]
        gmat = hp[0:1] * _softplus(sm + hp[1:2])
        beta = _sigmoid(sm)
        if n_valid < c:
            live = _iota((c, LANES), 0) < n_valid
            gmat = jnp.where(live, gmat, 0.0)
            beta = jnp.where(live, beta, 0.0)
        gmats.append(gmat)
        betas.append(beta)
    gc_alls = [_dot_exact_lhs(tri, g) for g in gmats]
    gr_alls = [_dot_nt_exact_lhs(sel, g) for g in gc_alls]

    q_b, k_b, kbs, rhs_uw, decays, egs, gcs, khs = [], [], [], [], [], [], [], []
    for s, h in chains:
        y = ys[s]
        qh = y[:, h * GDN_DK:(h + 1) * GDN_DK]
        kh = y[:, GDN_QK + h * GDN_DK:GDN_QK + (h + 1) * GDN_DK]
        vh = y[:, 2 * GDN_QK + h * GDN_DV:2 * GDN_QK + (h + 1) * GDN_DV]
        qh = qh * lax.rsqrt(jnp.sum(qh * qh, axis=-1, keepdims=True) + EPS) * (GDN_DK ** -0.5)
        kh = kh * lax.rsqrt(jnp.sum(kh * kh, axis=-1, keepdims=True) + EPS)
        b_col = betas[s][:, SM_B + h:SM_B + h + 1]
        gc = gc_alls[s][:, SM_A + h:SM_A + h + 1]
        decays.append(jnp.where(incl, jnp.exp(jnp.where(incl, gc - gr_alls[s][h:h + 1, :], 0.0)), 0.0))
        eg = jnp.exp(gc)
        kb = kh * b_col
        q_b.append((qh.astype(BF16), (qh * eg).astype(BF16)))
        k_b.append(kh.astype(BF16))
        kbs.append(kb.astype(BF16))
        rhs_uw.append(jnp.concatenate([(vh * b_col).astype(BF16), (kb * eg).astype(BF16)], axis=1))
        egs.append(eg)
        gcs.append(gc)
        khs.append(kh)

    n = len(chains)
    kkts = [_dot_nt(kbs[i], k_b[i]) * decays[i] for i in range(n)]
    qks = [(_dot_nt(q_b[i][0], k_b[i]) * decays[i]).astype(BF16) for i in range(n)]
    ps = [-jnp.where(ri > ci, kk, 0.0) for kk in kkts]
    ts = [eye + p for p in ps]
    for _ in range(max(1, int(math.ceil(math.log2(c)))) - 1):
        ps = [_dot_hp(p, p) for p in ps]
        ts = [t + _dot_hp(t, p) for t, p in zip(ts, ps)]
    uws = [_dot(ts[i].astype(BF16), rhs_uw[i]) for i in range(n)]
    s_olds = [s_ref[s, h] for s, h in chains]
    s_bs = [x.astype(BF16) for x in s_olds]
    v_new_b = [(uws[i][:, :GDN_DV] - _dot(uws[i][:, GDN_DV:].astype(BF16), s_bs[i])).astype(BF16)
               for i in range(n)]
    outs = [_dot_pieces([q_b[i][1], qks[i]], [s_bs[i], v_new_b[i]]) for i in range(n)]
    for i, (s, h) in enumerate(chains):
        g_last = gcs[i][c - 1:c, :]
        k_dec = khs[i] * jnp.exp(g_last - gcs[i])
        s_ref[s, h] = s_olds[i] * jnp.exp(g_last) + _dot_tn(k_dec.astype(BF16), v_new_b[i])
    for i, (s, h) in enumerate(chains):
        o = outs[i]
        on = o * lax.rsqrt(jnp.mean(o * o, axis=-1, keepdims=True) + EPS) * ng_ref[...]
        zh = z_ref[s, :, h * GDN_DV:(h + 1) * GDN_DV]
        o_ref[s, :, h * GDN_DV:(h + 1) * GDN_DV] = on * _silu(zh)
    sout_ref[...] = s_ref[...]


def _gdn(proj, conv0, s0, conv_w, a_log, dt_bias, norm_g, n_seq, rows_per_seq, chunk, n_valid):
    t, n_proj = proj.shape
    nc = rows_per_seq // chunk
    bg = next(g for g in ((2, 1) if nc > 1 else (4, 2, 1)) if n_seq % g == 0)
    p3 = proj.reshape(n_seq, rows_per_seq, n_proj)
    hp = jnp.zeros((ROWPAD, LANES), F32)
    hp = hp.at[0, SM_A:SM_A + GDN_HEADS].set(-jnp.exp(a_log))
    hp = hp.at[1, SM_A:SM_A + GDN_HEADS].set(dt_bias)
    kern = functools.partial(_gdn_kernel, chunk=chunk, n_valid=n_valid)
    state_spec = pl.BlockSpec((bg, GDN_HEADS, GDN_DK, GDN_DV), lambda g, c: (g, 0, 0, 0))
    go, s_new = pl.pallas_call(
        kern,
        out_shape=(jax.ShapeDtypeStruct((n_seq, rows_per_seq, GDN_V), F32),
                   jax.ShapeDtypeStruct((n_seq, GDN_HEADS, GDN_DK, GDN_DV), F32)),
        grid=(n_seq // bg, nc),
        in_specs=[pl.BlockSpec((bg, chunk, GDN_CH), lambda g, c: (g, c, C_QKV // GDN_CH)),
                  pl.BlockSpec((bg, chunk, LANES), lambda g, c: (g, c, n_proj // LANES - 1)),
                  pl.BlockSpec((bg, chunk, GDN_V), lambda g, c: (g, c, C_Z // GDN_V)),
                  pl.BlockSpec((bg, ROWPAD, GDN_CH), lambda g, c: (g, 0, 0)),
                  state_spec,
                  pl.BlockSpec((GDN_CONV, GDN_CH), lambda g, c: (0, 0)),
                  pl.BlockSpec((ROWPAD, LANES), lambda g, c: (0, 0)),
                  pl.BlockSpec((1, GDN_DV), lambda g, c: (0, 0))],
        out_specs=(pl.BlockSpec((bg, chunk, GDN_V), lambda g, c: (g, c, 0)), state_spec),
        scratch_shapes=[pltpu.VMEM((bg, ROWPAD + chunk, GDN_CH), F32),
                        pltpu.VMEM((bg, GDN_HEADS, GDN_DK, GDN_DV), F32)],
        compiler_params=_cparams(("arbitrary", "arbitrary")),
        name="gdn_scan",
    )(p3, p3, p3, conv0, s0, conv_w, hp, norm_g.reshape(1, GDN_DV))
    return go.reshape(t, GDN_V), s_new


def _fox_pre_kernel(sm_ref, b_ref, logf_ref, f_ref, ft_ref, carry):
    @pl.when(pl.program_id(1) == 0)
    def _():
        carry[...] = jnp.zeros_like(carry)

    tm = sm_ref.shape[0]
    lf = _log_sigmoid(sm_ref[...] + b_ref[...])
    logf_ref[...] = lf
    tri = jnp.where(_iota((tm, tm), 0) >= _iota((tm, tm), 1), 1.0, 0.0).astype(BF16)
    cs = _dot_exact_lhs(tri, lf) + carry[...]
    f_ref[...] = cs
    carry[...] = cs[tm - 1:tm, :]
    sel = jnp.where(_iota((ROWPAD, LANES), 0) == _iota((ROWPAD, LANES), 1), 1.0, 0.0).astype(BF16)
    ft_ref[0] = _dot_nt_exact_lhs(sel, cs)


def _fox_pre(proj, f_bias, n_seq, rows_per_seq, tm):
    t = proj.shape[0]
    nb = rows_per_seq // tm
    brow = jnp.zeros((1, LANES), F32).at[0, SM_F:SM_F + FOX_HEADS].set(f_bias)
    return pl.pallas_call(
        _fox_pre_kernel,
        out_shape=(jax.ShapeDtypeStruct((t, LANES), F32),
                   jax.ShapeDtypeStruct((t, LANES), F32),
                   jax.ShapeDtypeStruct((n_seq, ROWPAD, rows_per_seq), F32)),
        grid=(n_seq, nb),
        in_specs=[pl.BlockSpec((tm, LANES), lambda b, i: (b * nb + i, proj.shape[1] // LANES - 1)),
                  pl.BlockSpec((1, LANES), lambda b, i: (0, 0))],
        out_specs=(pl.BlockSpec((tm, LANES), lambda b, i: (b * nb + i, 0)),
                   pl.BlockSpec((tm, LANES), lambda b, i: (b * nb + i, 0)),
                   pl.BlockSpec((1, ROWPAD, tm), lambda b, i: (b, 0, i))),
        scratch_shapes=[pltpu.VMEM((1, LANES), F32)],
        compiler_params=_cparams(("arbitrary", "arbitrary")),
        name="fox_pre",
    )(proj, brow)


def _head_mask(h, dh):
    return (_iota((1, LANES), 1) // dh) == h


def _sb_prompt_kernel(q_ref, k_ref, v_ref, u_ref, o_ref, *, tq, tk):
    i = pl.program_id(2)
    nd = tq // tk
    q2 = q_ref[...] * (SB_DH ** -0.5)
    masks = [_head_mask(h, SB_DH) for h in range(2)]
    qhs = [jnp.where(m, q2, 0.0).astype(BF16) for m in masks]
    ustrict = u_ref[...]

    def scores(j, causal=False, optional=False):
        jc = jnp.maximum(j, 0) if optional else j
        start = pl.multiple_of(jc * tk, tk)
        kb = k_ref[pl.ds(start, tk), :].astype(BF16)
        vfull = v_ref[pl.ds(start, tk), :]
        earlier = present = None
        if causal:
            earlier = (j * tk + _iota((tq, tk), 1)) < (i * tq + _iota((tq, tk), 0))
        if optional:
            present = jnp.where(j >= 0, 1.0, 0.0)
        ss = [_dot_nt(qhs[h], kb) for h in range(2)]
        sps = [_softplus(s) for s in ss]
        lfs = []
        for h in range(2):
            lf = -sps[h]
            if causal:
                lf = jnp.where(earlier, lf, 0.0)
            if optional:
                lf = lf * present
            lfs.append(lf)
        cums = []
        for h in range(2):
            hi, lo = _split2(lfs[h])
            cums.append(_dot_pieces([hi, lo], [ustrict, ustrict]))
        vbs = [jnp.where(masks[h], vfull, 0.0).astype(BF16) for h in range(2)]
        logw = [ss[h] - sps[h] + cums[h] for h in range(2)]
        tots = [cums[h][:, 0:1] + lfs[h][:, 0:1] for h in range(2)]
        return logw, tots, vbs, earlier, present

    def apply(pack, carries, acc):
        logw, tots, vbs, earlier, present = pack
        out_c = []
        for h in range(2):
            a = jnp.exp(logw[h] + carries[h])
            if earlier is not None:
                a = jnp.where(earlier, a, 0.0)
            if present is not None:
                a = a * present
            acc = acc + _dot(a.astype(BF16), vbs[h])
            out_c.append(carries[h] + tots[h])
        return out_c, acc

    def block(j, carries, acc):
        return apply(scores(j), carries, acc)

    carries = [jnp.zeros((tq, 1), F32), jnp.zeros((tq, 1), F32)]
    acc = jnp.zeros((tq, LANES), F32)
    packs = [scores(i * nd + (nd - 1 - d), causal=True) for d in range(nd)]
    packs += [scores(i * nd - 1 - d, optional=True) for d in range(SB_EAGER_BLOCKS)]
    for pack in packs:
        carries, acc = apply(pack, carries, acc)

    def cond(st):
        j, c0, c1, _ = st
        return jnp.logical_and(j >= 0, jnp.maximum(jnp.max(c0), jnp.max(c1)) > UNDERFLOW)

    def body(st):
        j, c0, c1, a = st
        (c0, c1), a = block(j, [c0, c1], a)
        return j - 1, c0, c1, a

    st = lax.while_loop(cond, body, (i * nd - 1 - SB_EAGER_BLOCKS, carries[0], carries[1], acc))
    o_ref[...] = st[3]


def _sb_prompt(proj, n_seq, seq, tq, tk):
    t = proj.shape[0]
    nq = seq // tq
    cq, ck, cv = C_SB // LANES, (C_SB + SB_W) // LANES, (C_SB + 2 * SB_W) // LANES
    ustrict = (jnp.arange(tk)[:, None] > jnp.arange(tk)[None, :]).astype(BF16)
    return pl.pallas_call(
        functools.partial(_sb_prompt_kernel, tq=tq, tk=tk),
        out_shape=jax.ShapeDtypeStruct((t, SB_W), F32),
        grid=(n_seq, SB_HEADS // 2, nq),
        in_specs=[pl.BlockSpec((tq, LANES), lambda b, p, i: (b * nq + i, cq + p)),
                  pl.BlockSpec((seq, LANES), lambda b, p, i: (b, ck + p)),
                  pl.BlockSpec((seq, LANES), lambda b, p, i: (b, cv + p)),
                  pl.BlockSpec((tk, tk), lambda b, p, i: (0, 0))],
        out_specs=pl.BlockSpec((tq, LANES), lambda b, p, i: (b * nq + i, p)),
        compiler_params=_cparams(("parallel", "parallel", "arbitrary")),
        name="sb_prompt",
    )(proj, proj, proj, ustrict)


def _fox_prompt_kernel(q_ref, k_ref, v_ref, f_ref, ft_ref, o_ref, kmax_ref, *, tq, tk):
    p = pl.program_id(1)
    i = pl.program_id(2)
    nd = tq // tk
    q2 = q_ref[...] * (FOX_DH ** -0.5)
    masks = [_head_mask(h, FOX_DH) for h in range(2)]
    qhs = [jnp.where(m, q2, 0.0).astype(BF16) for m in masks]
    fblk = f_ref[...]
    lane = _iota((1, LANES), 1)
    fq = [jnp.sum(jnp.where(lane == 2 * p + h, fblk, 0.0), axis=1, keepdims=True) for h in range(2)]

    def block(j, m, l, acc, masked):
        start = pl.multiple_of(j * tk, tk)
        kb = k_ref[pl.ds(start, tk), :].astype(BF16)
        vfull = v_ref[pl.ds(start, tk), :]
        if masked:
            visible = (j * tk + _iota((tq, tk), 1)) <= (i * tq + _iota((tq, tk), 0))
        ftb = ft_ref[0, :, pl.ds(start, tk)]
        hrow = _iota((ROWPAD, tk), 0)
        m2, l2, a2, f2 = [], [], [], []
        for h in range(2):
            fk = jnp.sum(jnp.where(hrow == 2 * p + h, ftb, 0.0), axis=0, keepdims=True)
            s = _dot_nt(qhs[h], kb) + (fq[h] - fk)
            if masked:
                s = jnp.where(visible, s, -jnp.inf)
            m_new = jnp.maximum(m[h], jnp.max(s, axis=1, keepdims=True))
            alpha = jnp.exp(m[h] - m_new)
            pr = jnp.exp(s - m_new)
            vb = jnp.where(masks[h], vfull, 0.0).astype(BF16)
            l2.append(l[h] * alpha + jnp.sum(pr, axis=1, keepdims=True))
            a2.append(acc[h] * alpha + _dot(pr.astype(BF16), vb))
            m2.append(m_new)
            f2.append(fk[:, 0:1])
        return m2, l2, a2, f2

    @pl.when(i == 0)
    def _():
        def norm_body(j, best):
            kk = k_ref[pl.ds(pl.multiple_of(j * tk, tk), tk), :]
            k2 = kk * kk
            return tuple(jnp.maximum(best[h],
                                     jnp.max(jnp.sum(jnp.where(masks[h], k2, 0.0), axis=1, keepdims=True)))
                         for h in range(2))
        best = lax.fori_loop(0, k_ref.shape[0] // tk, norm_body, (jnp.float32(0.0), jnp.float32(0.0)))
        kmax_ref[0] = best[0]
        kmax_ref[1] = best[1]

    qk_cap = [jnp.sqrt(jnp.sum(jnp.where(masks[h], q2 * q2, 0.0), axis=1, keepdims=True) * kmax_ref[h])
              for h in range(2)]

    neg = jnp.full((tq, 1), -jnp.inf, F32)
    zc = jnp.zeros((tq, 1), F32)
    za = jnp.zeros((tq, LANES), F32)
    m, l, a = [neg, neg], [zc, zc], [za, za]
    f_first = None
    for d in range(nd):
        m, l, a, f = block(i * nd + d, m, l, a, True)
        f_first = f if f_first is None else f_first

    def cond(st):
        j, m0, m1 = st[0], st[1], st[2]
        f0, f1 = st[7], st[8]
        slack = jnp.maximum(jnp.max(qk_cap[0] + fq[0] - f0 - m0), jnp.max(qk_cap[1] + fq[1] - f1 - m1))
        return jnp.logical_and(j >= 0, slack > UNDERFLOW)

    def body(st):
        j, m0, m1, l0, l1, a0, a1, _, _ = st
        m, l, a, f = block(j, [m0, m1], [l0, l1], [a0, a1], False)
        return j - 1, m[0], m[1], l[0], l[1], a[0], a[1], f[0], f[1]

    st = lax.while_loop(cond, body, (i * nd - 1, m[0], m[1], l[0], l[1], a[0], a[1], f_first[0], f_first[1]))
    o_ref[...] = st[5] * (1.0 / st[3]) + st[6] * (1.0 / st[4])


def _fox_prompt(proj, fsum, fsum_t, n_seq, seq, tq, tk):
    t = proj.shape[0]
    nq = seq // tq
    cq, ck, cv = C_FOX // LANES, (C_FOX + FOX_W) // LANES, (C_FOX + 2 * FOX_W) // LANES
    return pl.pallas_call(
        functools.partial(_fox_prompt_kernel, tq=tq, tk=tk),
        out_shape=jax.ShapeDtypeStruct((t, FOX_W), F32),
        grid=(n_seq, FOX_HEADS // 2, nq),
        in_specs=[pl.BlockSpec((tq, LANES), lambda b, p, i: (b * nq + i, cq + p)),
                  pl.BlockSpec((seq, LANES), lambda b, p, i: (b, ck + p)),
                  pl.BlockSpec((seq, LANES), lambda b, p, i: (b, cv + p)),
                  pl.BlockSpec((tq, LANES), lambda b, p, i: (b * nq + i, 0)),
                  pl.BlockSpec((1, ROWPAD, seq), lambda b, p, i: (b, 0, 0))],
        out_specs=pl.BlockSpec((tq, LANES), lambda b, p, i: (b * nq + i, p)),
        scratch_shapes=[pltpu.SMEM((2,), F32)],
        compiler_params=_cparams(("arbitrary", "arbitrary", "arbitrary")),
        name="fox_prompt",
    )(proj, proj, proj, fsum, fsum_t)


def _block_diag_q(q, heads, dh):
    shape = (heads * ROWPAD, heads * dh)
    qt = jnp.concatenate([q] * heads, axis=0) * (dh ** -0.5)
    own = (_iota(shape, 0) // ROWPAD) == (_iota(shape, 1) // dh)
    return jnp.where(own, qt, 0.0).astype(BF16)


def _diag_rows(acc, heads, dh):
    own = (_iota(acc.shape, 0) // ROWPAD) == (_iota(acc.shape, 1) // dh)
    m = jnp.where(own, acc, 0.0)
    out = m[0:ROWPAD]
    for h in range(1, heads):
        out = out + m[h * ROWPAD:(h + 1) * ROWPAD]
    return out


def _lane_suffix(x):
    n = x.shape[-1]
    ax = len(x.shape) - 1
    lane = _iota(x.shape, ax)
    acc = x
    d = 1
    while d < n:
        shifted = pltpu.roll(acc, n - d, ax)
        acc = acc + jnp.where(lane < n - d, shifted, 0.0)
        d *= 2
    return acc - x, acc


def _expand_heads(x):
    return jnp.concatenate([jnp.broadcast_to(x[h:h + 1], (ROWPAD, x.shape[1])) for h in range(x.shape[0])],
                           axis=0)


def _page_bf16(page):
    return page.reshape(page.shape[0] * page.shape[1], page.shape[2]).astype(BF16)


def _sb_page(qbd, kpage, vpage, carry, visible):
    vf = _page_bf16(vpage)
    s = _dot(qbd, _page_bf16(kpage))
    sp = _softplus(s)
    lf = -sp
    if visible is not None:
        lf = jnp.where(visible, lf, 0.0)
    later, incl = _lane_suffix(lf)
    a = jnp.exp(s - sp + carry + later)
    if visible is not None:
        a = jnp.where(visible, a, 0.0)
    return _dot_nt(a.astype(BF16), vf), carry + incl[:, 0:1]


def _sb_decode_kernel(pt_ref, q_ref, kn_ref, vn_ref, k_hbm, v_hbm, o_ref, kbuf, vbuf, sem, *, layer, n_pages):
    b = pl.program_id(0)
    heads, dh = SB_HEADS, SB_DH
    rows = heads * ROWPAD
    qbd = _block_diag_q(q_ref[...], heads, dh)

    def copies(p, slot):
        page = pt_ref[b, p]
        return (pltpu.make_async_copy(k_hbm.at[layer, page], kbuf.at[slot], sem.at[0, slot]),
                pltpu.make_async_copy(v_hbm.at[layer, page], vbuf.at[slot], sem.at[1, slot]))

    def start(p, slot):
        for cp in copies(p, slot):
            cp.start()

    def wait(p, slot):
        for cp in copies(p, slot):
            cp.wait()

    start(n_pages - 1, 0)
    visible = _iota((rows, PAGE), 1) < (_iota((rows, PAGE), 0) % ROWPAD)
    acc, carry = _sb_page(qbd, kn_ref[0], vn_ref[0], jnp.zeros((rows, 1), F32), visible)

    def cond(st):
        p, cr, _ = st
        return jnp.logical_and(p >= 0, jnp.max(cr) > UNDERFLOW)

    def body(st):
        p, cr, ac = st
        slot = (n_pages - 1 - p) % 2
        wait(p, slot)

        @pl.when(p > 0)
        def _():
            start(p - 1, 1 - slot)

        d, cr = _sb_page(qbd, kbuf[slot], vbuf[slot], cr, None)
        return p - 1, cr, ac + d

    p_exit, carry, acc = lax.while_loop(cond, body, (n_pages - 1, carry, acc))

    @pl.when(p_exit >= 0)
    def _():
        wait(p_exit, (n_pages - 1 - p_exit) % 2)

    o_ref[...] = _diag_rows(acc, heads, dh)


def _sb_decode(page_table, proj, kn, vn, cache_k, cache_v, layer):
    n_seq, n_pages = page_table.shape
    heads, dh = SB_HEADS, SB_DH
    page_shape = (heads, dh, PAGE)
    new_spec = pl.BlockSpec((1,) + page_shape, lambda b, pt: (b, 0, 0, 0))
    grid_spec = pltpu.PrefetchScalarGridSpec(
        num_scalar_prefetch=1,
        grid=(n_seq,),
        in_specs=[pl.BlockSpec((ROWPAD, SB_W), lambda b, pt: (b, C_SB // SB_W)),
                  new_spec, new_spec,
                  pl.BlockSpec(memory_space=pl.ANY), pl.BlockSpec(memory_space=pl.ANY)],
        out_specs=pl.BlockSpec((ROWPAD, SB_W), lambda b, pt: (b, 0)),
        scratch_shapes=[pltpu.VMEM((2,) + page_shape, F32), pltpu.VMEM((2,) + page_shape, F32),
                        pltpu.SemaphoreType.DMA((2, 2))],
    )
    return pl.pallas_call(
        functools.partial(_sb_decode_kernel, layer=layer, n_pages=n_pages),
        out_shape=jax.ShapeDtypeStruct((n_seq * ROWPAD, SB_W), F32),
        grid_spec=grid_spec,
        compiler_params=_cparams(("arbitrary",)),
        name="sb_decode",
    )(page_table, proj, kn, vn, cache_k, cache_v)


def _page_suffix_kernel(x_ref, later_ref, incl_ref):
    x = x_ref[...]
    n = x.shape[1]
    ustrict = jnp.where(_iota((n, n), 0) > _iota((n, n), 1), 1.0, 0.0).astype(BF16)
    later = _dot_exact_rhs(x, ustrict)
    later_ref[...] = later
    incl_ref[...] = later + x


def _page_suffix(lf_pages):
    shape = lf_pages.shape
    rows = math.prod(shape[:-1])
    tm = _tile(rows, 2048)
    spec = pl.BlockSpec((tm, shape[-1]), lambda i: (i, 0))
    later, incl = pl.pallas_call(
        _page_suffix_kernel,
        out_shape=(jax.ShapeDtypeStruct((rows, shape[-1]), F32),) * 2,
        grid=(rows // tm,),
        in_specs=[spec],
        out_specs=(spec, spec),
        compiler_params=_cparams(("parallel",)),
        name="page_suffix",
    )(lf_pages.reshape(rows, shape[-1]))
    return later.reshape(shape), incl.reshape(shape)


def _fox_decode_kernel(pt_ref, q_ref, kn_ref, vn_ref, pc_ref, pn_ref, k_hbm, v_hbm, lt_hbm, in_hbm, o_ref,
                       kbuf, vbuf, ltbuf, inbuf, sem, *, layer, n_pages, group):
    b = pl.program_id(0)
    heads, dh = FOX_HEADS, FOX_DH
    rows = heads * ROWPAD
    n_groups = n_pages // group
    qbd = _block_diag_q(q_ref[...], heads, dh)
    pcol = pc_ref[0]

    def copies(gi, slot):
        out = []
        for g in range(group):
            page = pt_ref[b, n_pages - 1 - gi * group - g]
            for a, (hbm, buf) in enumerate(((k_hbm, kbuf), (v_hbm, vbuf), (lt_hbm, ltbuf), (in_hbm, inbuf))):
                out.append(pltpu.make_async_copy(hbm.at[layer, page], buf.at[slot, g], sem.at[a, slot]))
        return out

    def start(gi, slot):
        for cp in copies(gi, slot):
            cp.start()

    def wait(gi, slot):
        for cp in copies(gi, slot):
            cp.wait()

    start(0, 0)
    visible = _iota((rows, PAGE), 1) <= (_iota((rows, PAGE), 0) % ROWPAD)
    s = _dot(qbd, _page_bf16(kn_ref[0])) + (pcol - _expand_heads(pn_ref[0]))
    s = jnp.where(visible, s, -jnp.inf)
    m = jnp.max(s, axis=1, keepdims=True)
    pr = jnp.exp(s - m)
    l = jnp.sum(pr, axis=1, keepdims=True)
    acc = _dot_nt(pr.astype(BF16), _page_bf16(vn_ref[0]))

    def body(gi, st):
        m_old, l_old, acc_old, run = st
        slot = gi % 2
        wait(gi, slot)

        @pl.when(gi + 1 < n_groups)
        def _():
            start(gi + 1, 1 - slot)

        parts = []
        for g in range(group):
            bias = pcol + _expand_heads(run + ltbuf[slot, g])
            parts.append(_dot(qbd, _page_bf16(kbuf[slot, g])) + bias)
            run = run + inbuf[slot, g][:, 0:1]
        s = jnp.concatenate(parts, axis=1)
        m_new = jnp.maximum(m_old, jnp.max(s, axis=1, keepdims=True))
        alpha = jnp.exp(m_old - m_new)
        pr = jnp.exp(s - m_new)
        l_new = l_old * alpha + jnp.sum(pr, axis=1, keepdims=True)
        pv = _dot_nt(pr[:, 0:PAGE].astype(BF16), _page_bf16(vbuf[slot, 0]))
        for g in range(1, group):
            pv = pv + _dot_nt(pr[:, g * PAGE:(g + 1) * PAGE].astype(BF16), _page_bf16(vbuf[slot, g]))
        return m_new, l_new, acc_old * alpha + pv, run

    m, l, acc, _ = lax.fori_loop(0, n_groups, body, (m, l, acc, jnp.zeros((heads, PAGE), F32)))
    o_ref[...] = _diag_rows(acc * (1.0 / l), heads, dh)


def _fox_decode(page_table, proj, kn, vn, pcol, pnew, cache_k, cache_v, lf_later, lf_incl, layer):
    n_seq, n_pages = page_table.shape
    heads, dh = FOX_HEADS, FOX_DH
    rows = heads * ROWPAD
    page_shape = (heads, dh, PAGE)
    group = next(g for g in (16, 8, 4, 2, 1) if n_pages % g == 0)
    new_spec = pl.BlockSpec((1,) + page_shape, lambda b, pt: (b, 0, 0, 0))
    any_spec = pl.BlockSpec(memory_space=pl.ANY)
    grid_spec = pltpu.PrefetchScalarGridSpec(
        num_scalar_prefetch=1,
        grid=(n_seq,),
        in_specs=[pl.BlockSpec((ROWPAD, FOX_W), lambda b, pt: (b, C_FOX // FOX_W)),
                  new_spec, new_spec,
                  pl.BlockSpec((1, rows, 1), lambda b, pt: (b, 0, 0)),
                  pl.BlockSpec((1, heads, PAGE), lambda b, pt: (b, 0, 0)),
                  any_spec, any_spec, any_spec, any_spec],
        out_specs=pl.BlockSpec((ROWPAD, FOX_W), lambda b, pt: (b, 0)),
        scratch_shapes=[pltpu.VMEM((2, group) + page_shape, F32), pltpu.VMEM((2, group) + page_shape, F32),
                        pltpu.VMEM((2, group, heads, PAGE), F32), pltpu.VMEM((2, group, heads, PAGE), F32),
                        pltpu.SemaphoreType.DMA((4, 2))],
    )
    return pl.pallas_call(
        functools.partial(_fox_decode_kernel, layer=layer, n_pages=n_pages, group=group),
        out_shape=jax.ShapeDtypeStruct((n_seq * ROWPAD, FOX_W), F32),
        grid_spec=grid_spec,
        compiler_params=_cparams(("arbitrary",)),
        name="fox_decode",
    )(page_table, proj, kn, vn, pcol, pnew, cache_k, cache_v, lf_later, lf_incl)


def _merge_kernel(x_ref, go_ref, sb_ref, fx_ref, g0_ref, g1_ref, g2_ref, gt_ref,
                  wg_ref, ws_ref, wf_ref, wo_ref, o_ref):
    rows = x_ref.shape[0]
    a = _dot(go_ref[...].astype(BF16), wg_ref[...])
    b = _dot(sb_ref[...].astype(BF16), ws_ref[...])
    c = _dot(fx_ref[...].astype(BF16), wf_ref[...])
    merged = _sigmoid(g0_ref[...]) * a + _sigmoid(g1_ref[...]) * b + _sigmoid(g2_ref[...]) * c
    o_ref[...] = x_ref[...] + _mod_rows(gt_ref, rows) * _dot(merged.astype(BF16), wo_ref[...])


def _merge(x, go, sbo, foxo, proj, gt, wg, ws, wf, wo, rows_per_seq, tm):
    t, d = x.shape
    cg = C_GATE // d
    row = lambda w: pl.BlockSpec((tm, w), lambda i: (i, 0))
    full = lambda a: pl.BlockSpec(a.shape, lambda i: (0, 0))
    return pl.pallas_call(
        _merge_kernel,
        out_shape=jax.ShapeDtypeStruct((t, d), F32),
        grid=(t // tm,),
        in_specs=[row(d), row(GDN_V), row(SB_W), row(FOX_W),
                  pl.BlockSpec((tm, d), lambda i: (i, cg)),
                  pl.BlockSpec((tm, d), lambda i: (i, cg + 1)),
                  pl.BlockSpec((tm, d), lambda i: (i, cg + 2)),
                  _mod_spec(tm, rows_per_seq, d, False),
                  full(wg), full(ws), full(wf), full(wo)],
        out_specs=row(d),
        compiler_params=_cparams(("parallel",)),
        name="merge_out",
    )(x, go, sbo, foxo, proj, proj, proj, gt, wg, ws, wf, wo)


def _ffn_down_kernel(up_ref, prev_ref, x_ref, gt_ref, cw_ref, cb_ref, wd_ref, o_ref, xs, *, cwid):
    tm = up_ref.shape[0]
    d_ff = wd_ref.shape[0]
    halo = FFN_CONV - 1
    xs[0:ROWPAD, :] = prev_ref[0]
    xs[ROWPAD:ROWPAD + tm, :] = up_ref[...]

    def conv(c0):
        cs = slice(c0, c0 + cwid)
        y = cb_ref[:, cs] + xs[ROWPAD - halo:ROWPAD - halo + tm, cs] * cw_ref[0:1, cs]
        for i in range(1, FFN_CONV):
            y = y + xs[ROWPAD - halo + i:ROWPAD - halo + i + tm, cs] * cw_ref[i:i + 1, cs]
        return y

    acc = jnp.zeros((tm, o_ref.shape[1]), F32)
    for c0 in range(0, d_ff, cwid):
        hid = _silu(conv(c0)) * conv(d_ff + c0)
        acc = acc + _dot(hid.astype(BF16), wd_ref[c0:c0 + cwid, :])
    o_ref[...] = x_ref[...] + _mod_rows(gt_ref, tm) * acc


def _ffn_down(up, prev, x, gt, conv_w, conv_b, wd, rows_per_seq, tm):
    t, d = x.shape
    n2 = up.shape[1]
    d_ff = n2 // 2
    cwid = 256 if d_ff % 256 == 0 else LANES
    if rows_per_seq == ROWPAD:
        gt_spec = pl.BlockSpec((tm, d), lambda i: (i, 0))
    else:
        gt_spec = pl.BlockSpec((ROWPAD, d), lambda i: ((i * tm) // rows_per_seq, 0))
    return pl.pallas_call(
        functools.partial(_ffn_down_kernel, cwid=cwid),
        out_shape=jax.ShapeDtypeStruct((t, d), F32),
        grid=(t // tm,),
        in_specs=[pl.BlockSpec((tm, n2), lambda i: (i, 0)),
                  pl.BlockSpec((1, ROWPAD, n2), lambda i: (i, 0, 0)),
                  pl.BlockSpec((tm, d), lambda i: (i, 0)),
                  gt_spec,
                  pl.BlockSpec((FFN_CONV, n2), lambda i: (0, 0)),
                  pl.BlockSpec((1, n2), lambda i: (0, 0)),
                  pl.BlockSpec((d_ff, d), lambda i: (0, 0))],
        out_specs=pl.BlockSpec((tm, d), lambda i: (i, 0)),
        scratch_shapes=[pltpu.VMEM((ROWPAD + tm, n2), F32)],
        compiler_params=_cparams(("parallel",)),
        name="ffn_down",
    )(up, prev, x, gt, conv_w, conv_b.reshape(1, n2), wd)


def _final_norm_kernel(x_ref, g_ref, o_ref):
    x = x_ref[...]
    o_ref[...] = x * lax.rsqrt(jnp.mean(x * x, axis=-1, keepdims=True) + EPS) * g_ref[...]


def _final_norm(x, g, tm):
    t, d = x.shape
    return pl.pallas_call(
        _final_norm_kernel,
        out_shape=jax.ShapeDtypeStruct((t, d), F32),
        grid=(t // tm,),
        in_specs=[pl.BlockSpec((tm, d), lambda i: (i, 0)), pl.BlockSpec((1, d), lambda i: (0, 0))],
        out_specs=pl.BlockSpec((tm, d), lambda i: (i, 0)),
        compiler_params=_cparams(("parallel",)),
        name="final_norm",
    )(x, g.reshape(1, d))


def _tile(n, pref):
    t = min(pref, n)
    while n % t:
        t //= 2
    return t


def _prep_w_in(w_in, d):
    o_a = GDN_CH
    o_b = o_a + GDN_HEADS
    o_z = o_b + GDN_HEADS
    o_sb = o_z + GDN_V
    o_fox = o_sb + 3 * SB_W
    o_f = o_fox + 3 * FOX_W
    o_gate = o_f + FOX_HEADS
    small = jnp.concatenate([w_in[:, o_f:o_gate], w_in[:, o_a:o_b], w_in[:, o_b:o_z]], axis=1)
    small = jnp.pad(small, ((0, 0), (0, LANES - small.shape[1])))
    w = jnp.concatenate([w_in[:, :o_a], w_in[:, o_z:o_sb], w_in[:, o_sb:o_fox], w_in[:, o_fox:o_f],
                         w_in[:, o_gate:], small], axis=1)
    return w.astype(BF16)


def _rep(v):
    return jnp.repeat(v, ROWPAD, axis=0)


def _layer_group(x, mod, lw, rows_per_seq, n_valid, conv0, s0, ffn_prev_fn, attn_fn):
    t, d = x.shape
    n_seq = t // rows_per_seq
    sh1, sc1, gt1, sh2, sc2, gt2 = [_rep(m) for m in jnp.split(mod, 6, axis=-1)]
    tm = _tile(t if rows_per_seq == ROWPAD else rows_per_seq, 1024)
    n_proj = lw["w_in"].shape[1]
    proj = _norm_matmul(x, lw["norm1_g"], sc1, sh1, lw["w_in"], rows_per_seq, tm,
                        n_proj // 5, "proj_in")
    chunk = GDN_CHUNK if rows_per_seq % GDN_CHUNK == 0 else rows_per_seq
    go, s_new = _gdn(proj, conv0, s0, lw["gdn_conv_w"], lw["gdn_a_log"], lw["gdn_dt_bias"],
                     lw["gdn_norm_g"], n_seq, rows_per_seq, chunk, min(n_valid, chunk))
    logf, fsum, fsum_t = _fox_pre(proj, lw["fox_f_bias"], n_seq, rows_per_seq, _tile(rows_per_seq, 256))
    sbo, foxo = attn_fn(proj, fsum, fsum_t)
    x1 = _merge(x, go, sbo, foxo, proj, gt1, lw["w_br_gdn"], lw["w_br_sb"], lw["w_br_fox"], lw["w_out"],
                rows_per_seq, _tile(t, 256))
    n_up = lw["w_up"].shape[1]
    up = _norm_matmul(x1, lw["norm2_g"], sc2, sh2, lw["w_up"], rows_per_seq, tm, n_up // 4, "ffn_up")
    tm_d = ROWPAD if rows_per_seq == ROWPAD else _tile(rows_per_seq, 256)
    x2 = _ffn_down(up, ffn_prev_fn(up, tm_d), x1, gt2, lw["ffn_conv_w"], lw["ffn_conv_b"], lw["w_down"],
                   rows_per_seq, tm_d)
    return x2, proj, up, logf, s_new


def kernel(x_prompt, x_sample, cache_sb_k, cache_sb_v, cache_fox_k, cache_fox_v, cache_fox_logf, state_gdn, state_gdn_conv, state_ffn_conv, page_table, c_prompt, c_sample, norm1_g, norm2_g, final_g, w_ada, b_ada, w_in, gdn_conv_w, gdn_a_log, gdn_dt_bias, gdn_norm_g, fox_f_bias, w_br_gdn, w_br_sb, w_br_fox, w_out, w_up, ffn_conv_w, ffn_conv_b, w_down):
    bp, seq, d = x_prompt.shape
    bs, dseq, _ = x_sample.shape
    depth = w_in.shape[0]
    n2 = w_up.shape[2]
    assert dseq <= ROWPAD and seq % GDN_CHUNK == 0 and dseq >= GDN_CONV - 1

    xp = x_prompt.reshape(bp * seq, d)
    xs = jnp.pad(x_sample, ((0, 0), (0, ROWPAD - dseq), (0, 0))).reshape(bs * ROWPAD, d)
    n_c = bp + bs
    c_all = jnp.pad(jnp.concatenate([c_prompt, c_sample], axis=0), ((0, (-n_c) % ROWPAD), (0, 0)))
    sbk_pages, sbv_pages, fxk_pages, fxv_pages = [
        jnp.transpose(a, (0, 1, 3, 4, 2)) for a in (cache_sb_k, cache_sb_v, cache_fox_k, cache_fox_v)]
    lf_later, lf_incl = _page_suffix(jnp.transpose(cache_fox_logf, (0, 1, 3, 2)))

    def new_page(a, heads, dh):
        a = a.reshape(bs, ROWPAD, heads, dh).transpose(0, 2, 3, 1)
        return jnp.pad(a, ((0, 0), (0, 0), (0, 0), (0, PAGE - ROWPAD)))

    outs_p, outs_s = [], []
    for l in range(depth):
        lw = {"norm1_g": norm1_g[l], "norm2_g": norm2_g[l], "w_in": _prep_w_in(w_in[l], d),
              "gdn_conv_w": gdn_conv_w[l], "gdn_a_log": gdn_a_log[l], "gdn_dt_bias": gdn_dt_bias[l],
              "gdn_norm_g": gdn_norm_g[l], "fox_f_bias": fox_f_bias[l],
              "w_br_gdn": w_br_gdn[l].astype(BF16), "w_br_sb": w_br_sb[l].astype(BF16),
              "w_br_fox": w_br_fox[l].astype(BF16), "w_out": w_out[l].astype(BF16),
              "w_up": w_up[l].astype(BF16), "ffn_conv_w": ffn_conv_w[l], "ffn_conv_b": ffn_conv_b[l],
              "w_down": w_down[l].astype(BF16)}
        mod = _ada(c_all, w_ada[l], b_ada[l])

        def prompt_attn(proj, fsum, fsum_t):
            tq = _tile(seq, 256)
            tk = _tile(tq, 128)
            tf = _tile(seq, 512)
            return (_sb_prompt(proj, bp, seq, tq, tk), _fox_prompt(proj, fsum, fsum_t, bp, seq, tf, tf))

        def prompt_prev(up, tm_d):
            nt = up.shape[0] // tm_d
            tails = up.reshape(nt, tm_d, n2)[:, tm_d - ROWPAD:, :]
            prev = jnp.concatenate([jnp.zeros((1, ROWPAD, n2), F32), tails[:-1]], axis=0)
            first = (jnp.arange(nt) * tm_d) % seq == 0
            return jnp.where(first[:, None, None], 0.0, prev)

        xp, proj, up, logf, s_new = _layer_group(
            xp, mod[:bp], lw, seq, seq,
            jnp.zeros((bp, ROWPAD, GDN_CH), F32), jnp.zeros((bp, GDN_HEADS, GDN_DK, GDN_DV), F32),
            prompt_prev, prompt_attn)
        p3 = proj.reshape(bp, seq, -1)
        outs_p.append((
            p3[:, :, C_SB + SB_W:C_SB + 2 * SB_W].reshape(bp, seq, SB_HEADS, SB_DH),
            p3[:, :, C_SB + 2 * SB_W:C_SB + 3 * SB_W].reshape(bp, seq, SB_HEADS, SB_DH),
            p3[:, :, C_FOX + FOX_W:C_FOX + 2 * FOX_W].reshape(bp, seq, FOX_HEADS, FOX_DH),
            p3[:, :, C_FOX + 2 * FOX_W:C_FOX + 3 * FOX_W].reshape(bp, seq, FOX_HEADS, FOX_DH),
            logf.reshape(bp, seq, LANES)[:, :, SM_F:SM_F + FOX_HEADS],
            s_new,
            p3[:, seq - (GDN_CONV - 1):, C_QKV:C_QKV + GDN_CH],
            up.reshape(bp, seq, n2)[:, seq - (FFN_CONV - 1):, :]))

        def sample_attn(proj, fsum, fsum_t, l=l):
            def cols(c0, w):
                return proj[:, c0:c0 + w]
            sbo = _sb_decode(page_table, proj,
                             new_page(cols(C_SB + SB_W, SB_W), SB_HEADS, SB_DH),
                             new_page(cols(C_SB + 2 * SB_W, SB_W), SB_HEADS, SB_DH),
                             sbk_pages, sbv_pages, l)
            pcol = fsum_t.reshape(bs, FOX_HEADS * ROWPAD, 1)
            pnew = jnp.pad(fsum_t, ((0, 0), (0, 0), (0, PAGE - ROWPAD)))
            foxo = _fox_decode(page_table, proj,
                               new_page(cols(C_FOX + FOX_W, FOX_W), FOX_HEADS, FOX_DH),
                               new_page(cols(C_FOX + 2 * FOX_W, FOX_W), FOX_HEADS, FOX_DH),
                               pcol, pnew, fxk_pages, fxv_pages, lf_later, lf_incl, l)
            return sbo, foxo

        def sample_prev(up, tm_d, l=l):
            return jnp.pad(state_ffn_conv[l], ((0, 0), (ROWPAD - (FFN_CONV - 1), 0), (0, 0)))

        conv0 = jnp.pad(state_gdn_conv[l], ((0, 0), (ROWPAD - (GDN_CONV - 1), 0), (0, 0)))
        xs, proj, up, logf, s_new = _layer_group(
            xs, mod[bp:bp + bs], lw, ROWPAD, dseq, conv0, state_gdn[l], sample_prev, sample_attn)
        p3 = proj.reshape(bs, ROWPAD, -1)
        outs_s.append((
            p3[:, :dseq, C_SB + SB_W:C_SB + 2 * SB_W].reshape(bs, dseq, SB_HEADS, SB_DH),
            p3[:, :dseq, C_SB + 2 * SB_W:C_SB + 3 * SB_W].reshape(bs, dseq, SB_HEADS, SB_DH),
            p3[:, :dseq, C_FOX + FOX_W:C_FOX + 2 * FOX_W].reshape(bs, dseq, FOX_HEADS, FOX_DH),
            p3[:, :dseq, C_FOX + 2 * FOX_W:C_FOX + 3 * FOX_W].reshape(bs, dseq, FOX_HEADS, FOX_DH),
            logf.reshape(bs, ROWPAD, LANES)[:, :dseq, SM_F:SM_F + FOX_HEADS],
            s_new,
            p3[:, dseq - (GDN_CONV - 1):dseq, C_QKV:C_QKV + GDN_CH],
            up.reshape(bs, ROWPAD, n2)[:, dseq - (FFN_CONV - 1):dseq, :]))

    y_p = _final_norm(xp, final_g, _tile(xp.shape[0], 512)).reshape(bp, seq, d)
    y_s = _final_norm(xs, final_g, _tile(xs.shape[0], 512)).reshape(bs, ROWPAD, d)[:, :dseq]
    P = [jnp.stack([st[i] for st in outs_p]) for i in range(8)]
    S = [jnp.stack([st[i] for st in outs_s]) for i in range(8)]
    return (y_p, y_s, P[0], S[0], P[1], S[1], P[2], S[2], P[3], S[3],
            P[4], S[4], P[5], S[5], P[6], S[6], P[7], S[7])
```

```python
import functools
import math

import jax
import jax.numpy as jnp
from jax import lax
from jax.experimental import pallas as pl
from jax.experimental.pallas import tpu as pltpu

F32 = jnp.float32
BF16 = jnp.bfloat16

GDN_HEADS = 4
GDN_DK = 128
GDN_DV = 128
GDN_CONV = 4
GDN_CHUNK = 128
SB_HEADS = 8
SB_DH = 64
FOX_HEADS = 8
FOX_DH = 64
FFN_CONV = 3
EPS = 1e-6
PAGE = 128

GDN_QK = GDN_HEADS * GDN_DK
GDN_V = GDN_HEADS * GDN_DV
GDN_CH = 2 * GDN_QK + GDN_V
SB_W = SB_HEADS * SB_DH
FOX_W = FOX_HEADS * FOX_DH

C_QKV = 0
C_Z = 1536
C_SB = 2048
C_FOX = 3584
C_GATE = 5120
SM_F, SM_A, SM_B = 0, 8, 12

LANES = 128
ROWPAD = 8
UNDERFLOW = -104.0
SB_EAGER_BLOCKS = 2
VMEM_LIMIT = 56 * 1024 * 1024


def _cparams(sem):
    return pltpu.CompilerParams(dimension_semantics=sem, vmem_limit_bytes=VMEM_LIMIT)


def _dot(a, b):
    return lax.dot_general(a, b, (((1,), (0,)), ((), ())), preferred_element_type=F32)


def _dot_nt(a, b):
    return lax.dot_general(a, b, (((1,), (1,)), ((), ())), preferred_element_type=F32)


def _dot_tn(a, b):
    return lax.dot_general(a, b, (((0,), (0,)), ((), ())), preferred_element_type=F32)


def _split2(a):
    hi = a.astype(BF16)
    lo = (a - hi.astype(F32)).astype(BF16)
    return hi, lo


def _split3(a):
    hi = a.astype(BF16)
    r = a - hi.astype(F32)
    mid = r.astype(BF16)
    lo = (r - mid.astype(F32)).astype(BF16)
    return hi, mid, lo


def _dot_pieces(lhs, rhs):
    if all(a.shape[1] % LANES == 0 for a in lhs):
        return _dot(jnp.concatenate(lhs, axis=1), jnp.concatenate(rhs, axis=0))
    out = _dot(lhs[0], rhs[0])
    for a, b in zip(lhs[1:], rhs[1:]):
        out = out + _dot(a, b)
    return out


def _dot_exact_lhs(e, a):
    return _dot_pieces([e, e, e], list(_split3(a)))


def _dot_exact_rhs(a, e):
    return _dot_pieces(list(_split3(a)), [e, e, e])


def _dot_nt_exact_lhs(e, a):
    return _dot_nt(jnp.concatenate([e, e, e], axis=1), jnp.concatenate(_split3(a), axis=1))


def _dot_hp(a, b):
    ah, al = _split2(a)
    bh, bl = _split2(b)
    return _dot_pieces([ah, ah, al], [bh, bl, bh])


def _softplus(x):
    return jnp.maximum(x, 0.0) + jnp.log1p(jnp.exp(-jnp.abs(x)))


def _log_sigmoid(x):
    return jnp.minimum(x, 0.0) - jnp.log1p(jnp.exp(-jnp.abs(x)))


def _sigmoid(x):
    return 1.0 / (1.0 + jnp.exp(-x))


def _silu(x):
    return x * _sigmoid(x)


def _iota(shape, dim):
    return lax.broadcasted_iota(jnp.int32, shape, dim)


def _ada_kernel(c_ref, w_ref, b_ref, o_ref):
    c = c_ref[...]
    o_ref[...] = _dot(_silu(c).astype(BF16), w_ref[...].astype(BF16)) + b_ref[...]


def _ada(c_all, w_ada, b_ada):
    r, d = c_all.shape
    n = w_ada.shape[1]
    tn = d
    return pl.pallas_call(
        _ada_kernel,
        out_shape=jax.ShapeDtypeStruct((r, n), F32),
        grid=(n // tn,),
        in_specs=[pl.BlockSpec((r, d), lambda j: (0, 0)),
                  pl.BlockSpec((d, tn), lambda j: (0, j)),
                  pl.BlockSpec((1, tn), lambda j: (0, j))],
        out_specs=pl.BlockSpec((r, tn), lambda j: (0, j)),
        compiler_params=_cparams(("arbitrary",)),
        name="ada_mod",
    )(c_all, w_ada, b_ada.reshape(1, n))


def _mod_rows(ref, rows):
    v = ref[...]
    return v if v.shape[0] == rows else v[0:1]


def _norm_matmul_kernel(x_ref, g_ref, sc_ref, sh_ref, w_ref, o_ref, u_ref):
    @pl.when(pl.program_id(1) == 0)
    def _():
        x = x_ref[...]
        y = x * lax.rsqrt(jnp.mean(x * x, axis=-1, keepdims=True) + EPS) * g_ref[...]
        rows = x.shape[0]
        u_ref[...] = (y * (1.0 + _mod_rows(sc_ref, rows)) + _mod_rows(sh_ref, rows)).astype(BF16)

    o_ref[...] = _dot(u_ref[...], w_ref[...])


def _mod_spec(tm, rows_per_seq, d, ncols_grid):
    if rows_per_seq == ROWPAD:
        shape, imap = (tm, d), (lambda i: i)
    else:
        shape, imap = (ROWPAD, d), (lambda i: (i * tm) // rows_per_seq)
    if ncols_grid:
        return pl.BlockSpec(shape, lambda i, j: (imap(i), 0))
    return pl.BlockSpec(shape, lambda i: (imap(i), 0))


def _norm_matmul(x, g, sc, sh, w, rows_per_seq, tm, tn, name):
    t, d = x.shape
    n = w.shape[1]
    return pl.pallas_call(
        _norm_matmul_kernel,
        out_shape=jax.ShapeDtypeStruct((t, n), F32),
        grid=(t // tm, n // tn),
        in_specs=[pl.BlockSpec((tm, d), lambda i, j: (i, 0)),
                  pl.BlockSpec((1, d), lambda i, j: (0, 0)),
                  _mod_spec(tm, rows_per_seq, d, True),
                  _mod_spec(tm, rows_per_seq, d, True),
                  pl.BlockSpec((d, tn), lambda i, j: (0, j))],
        out_specs=pl.BlockSpec((tm, tn), lambda i, j: (i, j)),
        scratch_shapes=[pltpu.VMEM((tm, d), BF16)],
        compiler_params=_cparams(("parallel", "arbitrary")),
        name=name,
    )(x, g.reshape(1, d), sc, sh, w)


def _gdn_kernel(qkv_ref, sm_ref, z_ref, conv0_ref, s0_ref, cw_ref, hp_ref, ng_ref,
                o_ref, sout_ref, xbuf, s_ref, *, chunk, n_valid):
    @pl.when(pl.program_id(1) == 0)
    def _():
        xbuf[:, 0:ROWPAD, :] = conv0_ref[...]
        s_ref[...] = s0_ref[...]

    c = chunk
    halo = GDN_CONV - 1
    n_seq = qkv_ref.shape[0]
    chains = [(s, h) for s in range(n_seq) for h in range(GDN_HEADS)]
    ri = _iota((c, c), 0)
    ci = _iota((c, c), 1)
    incl = ri >= ci
    tri = jnp.where(incl, 1.0, 0.0).astype(BF16)
    eye = jnp.where(ri == ci, 1.0, 0.0)
    sel = jnp.where(_iota((ROWPAD, LANES), 1) == _iota((ROWPAD, LANES), 0) + SM_A, 1.0, 0.0).astype(BF16)
    cw = cw_ref[...]
    hp = hp_ref[...]

    ys, gmats, betas = [], [], []
    for s in range(n_seq):
        xb = xbuf.at[s]
        xb[ROWPAD:ROWPAD + c, :] = qkv_ref[s]
        y = xb[ROWPAD - halo:ROWPAD - halo + c, :] * cw[0:1]
        for i in range(1, GDN_CONV):
            y = y + xb[ROWPAD - halo + i:ROWPAD - halo + i + c, :] * cw[i:i + 1]
        xb[ROWPAD - halo:ROWPAD, :] = xb[ROWPAD - halo + c:ROWPAD + c, :]
        ys.append(_silu(y))
        sm = sm_ref[s]
        gmat = hp[0:1] * _softplus(sm + hp[1:2])
        beta = _sigmoid(sm)
        if n_valid < c:
            live = _iota((c, LANES), 0) < n_valid
            gmat = jnp.where(live, gmat, 0.0)
            beta = jnp.where(live, beta, 0.0)
        gmats.append(gmat)
        betas.append(beta)
    gc_alls = [_dot_exact_lhs(tri, g) for g in gmats]
    gr_alls = [_dot_nt_exact_lhs(sel, g) for g in gc_alls]

    q_b, k_b, kbs, rhs_uw, decays, egs, gcs, khs = [], [], [], [], [], [], [], []
    for s, h in chains:
        y = ys[s]
        qh = y[:, h * GDN_DK:(h + 1) * GDN_DK]
        kh = y[:, GDN_QK + h * GDN_DK:GDN_QK + (h + 1) * GDN_DK]
        vh = y[:, 2 * GDN_QK + h * GDN_DV:2 * GDN_QK + (h + 1) * GDN_DV]
        qh = qh * lax.rsqrt(jnp.sum(qh * qh, axis=-1, keepdims=True) + EPS) * (GDN_DK ** -0.5)
        kh = kh * lax.rsqrt(jnp.sum(kh * kh, axis=-1, keepdims=True) + EPS)
        b_col = betas[s][:, SM_B + h:SM_B + h + 1]
        gc = gc_alls[s][:, SM_A + h:SM_A + h + 1]
        decays.append(jnp.where(incl, jnp.exp(jnp.where(incl, gc - gr_alls[s][h:h + 1, :], 0.0)), 0.0))
        eg = jnp.exp(gc)
        kb = kh * b_col
        q_b.append((qh.astype(BF16), (qh * eg).astype(BF16)))
        k_b.append(kh.astype(BF16))
        kbs.append(kb.astype(BF16))
        rhs_uw.append(jnp.concatenate([(vh * b_col).astype(BF16), (kb * eg).astype(BF16)], axis=1))
        egs.append(eg)
        gcs.append(gc)
        khs.append(kh)

    n = len(chains)
    kkts = [_dot_nt(kbs[i], k_b[i]) * decays[i] for i in range(n)]
    qks = [(_dot_nt(q_b[i][0], k_b[i]) * decays[i]).astype(BF16) for i in range(n)]
    lo_bits = lambda x, s: jnp.bitwise_and(x, s - 1)
    base = min(ROWPAD, c)
    same = jnp.right_shift(ri, base.bit_length() - 1) == jnp.right_shift(ci, base.bit_length() - 1)
    ps = [-jnp.where(jnp.logical_and(same, ri > ci), kk, 0.0) for kk in kkts]
    ts = [eye + p for p in ps]
    for _ in range(base.bit_length() - 2):
        ps = [_dot_hp(p, p) for p in ps]
        ts = [t + _dot_hp(t, p) for t, p in zip(ts, ps)]
    blk = base
    while blk < c:
        pair = jnp.right_shift(ri, blk.bit_length()) == jnp.right_shift(ci, blk.bit_length())
        off = jnp.logical_and(pair, jnp.logical_and(lo_bits(ri, 2 * blk) >= blk, lo_bits(ci, 2 * blk) < blk))
        xs = [_dot_hp(jnp.where(off, kk, 0.0), t) for kk, t in zip(kkts, ts)]
        ts = [t - _dot_hp(t, x) for t, x in zip(ts, xs)]
        blk *= 2
    uws = [_dot(ts[i].astype(BF16), rhs_uw[i]) for i in range(n)]
    s_olds = [s_ref[s, h] for s, h in chains]
    s_bs = [x.astype(BF16) for x in s_olds]
    v_new_b = [(uws[i][:, :GDN_DV] - _dot(uws[i][:, GDN_DV:].astype(BF16), s_bs[i])).astype(BF16)
               for i in range(n)]
    outs = [_dot_pieces([q_b[i][1], qks[i]], [s_bs[i], v_new_b[i]]) for i in range(n)]
    for i, (s, h) in enumerate(chains):
        g_last = gcs[i][c - 1:c, :]
        k_dec = khs[i] * jnp.exp(g_last - gcs[i])
        s_ref[s, h] = s_olds[i] * jnp.exp(g_last) + _dot_tn(k_dec.astype(BF16), v_new_b[i])
    for i, (s, h) in enumerate(chains):
        o = outs[i]
        on = o * lax.rsqrt(jnp.mean(o * o, axis=-1, keepdims=True) + EPS) * ng_ref[...]
        zh = z_ref[s, :, h * GDN_DV:(h + 1) * GDN_DV]
        o_ref[s, :, h * GDN_DV:(h + 1) * GDN_DV] = on * _silu(zh)
    sout_ref[...] = s_ref[...]


def _gdn(proj, conv0, s0, conv_w, a_log, dt_bias, norm_g, n_seq, rows_per_seq, chunk, n_valid):
    t, n_proj = proj.shape
    nc = rows_per_seq // chunk
    bg = next(g for g in ((2, 1) if nc > 1 else (4, 2, 1)) if n_seq % g == 0)
    p3 = proj.reshape(n_seq, rows_per_seq, n_proj)
    hp = jnp.zeros((ROWPAD, LANES), F32)
    hp = hp.at[0, SM_A:SM_A + GDN_HEADS].set(-jnp.exp(a_log))
    hp = hp.at[1, SM_A:SM_A + GDN_HEADS].set(dt_bias)
    kern = functools.partial(_gdn_kernel, chunk=chunk, n_valid=n_valid)
    state_spec = pl.BlockSpec((bg, GDN_HEADS, GDN_DK, GDN_DV), lambda g, c: (g, 0, 0, 0))
    go, s_new = pl.pallas_call(
        kern,
        out_shape=(jax.ShapeDtypeStruct((n_seq, rows_per_seq, GDN_V), F32),
                   jax.ShapeDtypeStruct((n_seq, GDN_HEADS, GDN_DK, GDN_DV), F32)),
        grid=(n_seq // bg, nc),
        in_specs=[pl.BlockSpec((bg, chunk, GDN_CH), lambda g, c: (g, c, C_QKV // GDN_CH)),
                  pl.BlockSpec((bg, chunk, LANES), lambda g, c: (g, c, n_proj // LANES - 1)),
                  pl.BlockSpec((bg, chunk, GDN_V), lambda g, c: (g, c, C_Z // GDN_V)),
                  pl.BlockSpec((bg, ROWPAD, GDN_CH), lambda g, c: (g, 0, 0)),
                  state_spec,
                  pl.BlockSpec((GDN_CONV, GDN_CH), lambda g, c: (0, 0)),
                  pl.BlockSpec((ROWPAD, LANES), lambda g, c: (0, 0)),
                  pl.BlockSpec((1, GDN_DV), lambda g, c: (0, 0))],
        out_specs=(pl.BlockSpec((bg, chunk, GDN_V), lambda g, c: (g, c, 0)), state_spec),
        scratch_shapes=[pltpu.VMEM((bg, ROWPAD + chunk, GDN_CH), F32),
                        pltpu.VMEM((bg, GDN_HEADS, GDN_DK, GDN_DV), F32)],
        compiler_params=_cparams(("arbitrary", "arbitrary")),
        name="gdn_scan",
    )(p3, p3, p3, conv0, s0, conv_w, hp, norm_g.reshape(1, GDN_DV))
    return go.reshape(t, GDN_V), s_new


def _fox_pre_kernel(sm_ref, b_ref, logf_ref, f_ref, ft_ref, carry):
    @pl.when(pl.program_id(1) == 0)
    def _():
        carry[...] = jnp.zeros_like(carry)

    tm = sm_ref.shape[0]
    lf = _log_sigmoid(sm_ref[...] + b_ref[...])
    logf_ref[...] = lf
    tri = jnp.where(_iota((tm, tm), 0) >= _iota((tm, tm), 1), 1.0, 0.0).astype(BF16)
    cs = _dot_exact_lhs(tri, lf) + carry[...]
    f_ref[...] = cs
    carry[...] = cs[tm - 1:tm, :]
    sel = jnp.where(_iota((ROWPAD, LANES), 0) == _iota((ROWPAD, LANES), 1), 1.0, 0.0).astype(BF16)
    ft_ref[0] = _dot_nt_exact_lhs(sel, cs)


def _fox_pre(proj, f_bias, n_seq, rows_per_seq, tm):
    t = proj.shape[0]
    nb = rows_per_seq // tm
    brow = jnp.zeros((1, LANES), F32).at[0, SM_F:SM_F + FOX_HEADS].set(f_bias)
    return pl.pallas_call(
        _fox_pre_kernel,
        out_shape=(jax.ShapeDtypeStruct((t, LANES), F32),
                   jax.ShapeDtypeStruct((t, LANES), F32),
                   jax.ShapeDtypeStruct((n_seq, ROWPAD, rows_per_seq), F32)),
        grid=(n_seq, nb),
        in_specs=[pl.BlockSpec((tm, LANES), lambda b, i: (b * nb + i, proj.shape[1] // LANES - 1)),
                  pl.BlockSpec((1, LANES), lambda b, i: (0, 0))],
        out_specs=(pl.BlockSpec((tm, LANES), lambda b, i: (b * nb + i, 0)),
                   pl.BlockSpec((tm, LANES), lambda b, i: (b * nb + i, 0)),
                   pl.BlockSpec((1, ROWPAD, tm), lambda b, i: (b, 0, i))),
        scratch_shapes=[pltpu.VMEM((1, LANES), F32)],
        compiler_params=_cparams(("arbitrary", "arbitrary")),
        name="fox_pre",
    )(proj, brow)


def _head_mask(h, dh):
    return (_iota((1, LANES), 1) // dh) == h


def _sb_prompt_kernel(q_ref, k_ref, v_ref, u_ref, o_ref, *, tq, tk):
    i = pl.program_id(2)
    nd = tq // tk
    q2 = q_ref[...] * (SB_DH ** -0.5)
    masks = [_head_mask(h, SB_DH) for h in range(2)]
    qhs = [jnp.where(m, q2, 0.0).astype(BF16) for m in masks]
    ustrict = u_ref[...]

    def scores(j, causal=False, optional=False):
        jc = jnp.maximum(j, 0) if optional else j
        start = pl.multiple_of(jc * tk, tk)
        kb = k_ref[pl.ds(start, tk), :].astype(BF16)
        vfull = v_ref[pl.ds(start, tk), :]
        earlier = present = None
        if causal:
            earlier = (j * tk + _iota((tq, tk), 1)) < (i * tq + _iota((tq, tk), 0))
        if optional:
            present = jnp.where(j >= 0, 1.0, 0.0)
        ss = [_dot_nt(qhs[h], kb) for h in range(2)]
        sps = [_softplus(s) for s in ss]
        lfs = []
        for h in range(2):
            lf = -sps[h]
            if causal:
                lf = jnp.where(earlier, lf, 0.0)
            if optional:
                lf = lf * present
            lfs.append(lf)
        cums = []
        for h in range(2):
            hi, lo = _split2(lfs[h])
            cums.append(_dot_pieces([hi, lo], [ustrict, ustrict]))
        vbs = [jnp.where(masks[h], vfull, 0.0).astype(BF16) for h in range(2)]
        logw = [ss[h] - sps[h] + cums[h] for h in range(2)]
        tots = [cums[h][:, 0:1] + lfs[h][:, 0:1] for h in range(2)]
        return logw, tots, vbs, earlier, present

    def apply(pack, carries, acc):
        logw, tots, vbs, earlier, present = pack
        out_c = []
        for h in range(2):
            a = jnp.exp(logw[h] + carries[h])
            if earlier is not None:
                a = jnp.where(earlier, a, 0.0)
            if present is not None:
                a = a * present
            acc = acc + _dot(a.astype(BF16), vbs[h])
            out_c.append(carries[h] + tots[h])
        return out_c, acc

    def block(j, carries, acc):
        return apply(scores(j), carries, acc)

    carries = [jnp.zeros((tq, 1), F32), jnp.zeros((tq, 1), F32)]
    acc = jnp.zeros((tq, LANES), F32)
    packs = [scores(i * nd + (nd - 1 - d), causal=True) for d in range(nd)]
    packs += [scores(i * nd - 1 - d, optional=True) for d in range(SB_EAGER_BLOCKS)]
    for pack in packs:
        carries, acc = apply(pack, carries, acc)

    def cond(st):
        j, c0, c1, _ = st
        return jnp.logical_and(j >= 0, jnp.maximum(jnp.max(c0), jnp.max(c1)) > UNDERFLOW)

    def body(st):
        j, c0, c1, a = st
        (c0, c1), a = block(j, [c0, c1], a)
        return j - 1, c0, c1, a

    st = lax.while_loop(cond, body, (i * nd - 1 - SB_EAGER_BLOCKS, carries[0], carries[1], acc))
    o_ref[...] = st[3]


def _sb_prompt(proj, n_seq, seq, tq, tk):
    t = proj.shape[0]
    nq = seq // tq
    cq, ck, cv = C_SB // LANES, (C_SB + SB_W) // LANES, (C_SB + 2 * SB_W) // LANES
    ustrict = (jnp.arange(tk)[:, None] > jnp.arange(tk)[None, :]).astype(BF16)
    return pl.pallas_call(
        functools.partial(_sb_prompt_kernel, tq=tq, tk=tk),
        out_shape=jax.ShapeDtypeStruct((t, SB_W), F32),
        grid=(n_seq, SB_HEADS // 2, nq),
        in_specs=[pl.BlockSpec((tq, LANES), lambda b, p, i: (b * nq + i, cq + p)),
                  pl.BlockSpec((seq, LANES), lambda b, p, i: (b, ck + p)),
                  pl.BlockSpec((seq, LANES), lambda b, p, i: (b, cv + p)),
                  pl.BlockSpec((tk, tk), lambda b, p, i: (0, 0))],
        out_specs=pl.BlockSpec((tq, LANES), lambda b, p, i: (b * nq + i, p)),
        compiler_params=_cparams(("parallel", "parallel", "arbitrary")),
        name="sb_prompt",
    )(proj, proj, proj, ustrict)


def _fox_prompt_kernel(q_ref, k_ref, v_ref, f_ref, ft_ref, o_ref, kmax_ref, *, tq, tk):
    p = pl.program_id(1)
    i = pl.program_id(2)
    nd = tq // tk
    q2 = q_ref[...] * (FOX_DH ** -0.5)
    masks = [_head_mask(h, FOX_DH) for h in range(2)]
    qhs = [jnp.where(m, q2, 0.0).astype(BF16) for m in masks]
    fblk = f_ref[...]
    lane = _iota((1, LANES), 1)
    fq = [jnp.sum(jnp.where(lane == 2 * p + h, fblk, 0.0), axis=1, keepdims=True) for h in range(2)]

    def block(j, m, l, acc, masked):
        start = pl.multiple_of(j * tk, tk)
        kb = k_ref[pl.ds(start, tk), :].astype(BF16)
        vfull = v_ref[pl.ds(start, tk), :]
        if masked:
            visible = (j * tk + _iota((tq, tk), 1)) <= (i * tq + _iota((tq, tk), 0))
        ftb = ft_ref[0, :, pl.ds(start, tk)]
        hrow = _iota((ROWPAD, tk), 0)
        m2, l2, a2, f2 = [], [], [], []
        raw = [_dot_nt(qhs[h], kb) for h in range(2)]
        for h in range(2):
            fk = jnp.sum(jnp.where(hrow == 2 * p + h, ftb, 0.0), axis=0, keepdims=True)
            s = raw[h] + (fq[h] - fk)
            if masked:
                s = jnp.where(visible, s, -jnp.inf)
            m_new = jnp.maximum(m[h], jnp.max(s, axis=1, keepdims=True))
            alpha = jnp.exp(m[h] - m_new)
            pr = jnp.exp(s - m_new)
            vb = jnp.where(masks[h], vfull, 0.0).astype(BF16)
            l2.append(l[h] * alpha + jnp.sum(pr, axis=1, keepdims=True))
            a2.append(acc[h] * alpha + _dot(pr.astype(BF16), vb))
            m2.append(m_new)
            f2.append(fk[:, 0:1])
        return m2, l2, a2, f2

    @pl.when(i == 0)
    def _():
        def norm_body(j, best):
            kk = k_ref[pl.ds(pl.multiple_of(j * tk, tk), tk), :]
            k2 = kk * kk
            return tuple(jnp.maximum(best[h],
                                     jnp.max(jnp.sum(jnp.where(masks[h], k2, 0.0), axis=1, keepdims=True)))
                         for h in range(2))
        best = lax.fori_loop(0, k_ref.shape[0] // tk, norm_body, (jnp.float32(0.0), jnp.float32(0.0)))
        kmax_ref[0] = best[0]
        kmax_ref[1] = best[1]

    qk_cap = [jnp.sqrt(jnp.sum(jnp.where(masks[h], q2 * q2, 0.0), axis=1, keepdims=True) * kmax_ref[h])
              for h in range(2)]

    neg = jnp.full((tq, 1), -jnp.inf, F32)
    zc = jnp.zeros((tq, 1), F32)
    za = jnp.zeros((tq, LANES), F32)
    m, l, a = [neg, neg], [zc, zc], [za, za]
    f_first = None
    for d in range(nd):
        m, l, a, f = block(i * nd + d, m, l, a, True)
        f_first = f if f_first is None else f_first

    def cond(st):
        j, m0, m1 = st[0], st[1], st[2]
        f0, f1 = st[7], st[8]
        slack = jnp.maximum(jnp.max(qk_cap[0] + fq[0] - f0 - m0), jnp.max(qk_cap[1] + fq[1] - f1 - m1))
        return jnp.logical_and(j >= 0, slack > UNDERFLOW)

    def body(st):
        j, m0, m1, l0, l1, a0, a1, _, _ = st
        m, l, a, f = block(j, [m0, m1], [l0, l1], [a0, a1], False)
        return j - 1, m[0], m[1], l[0], l[1], a[0], a[1], f[0], f[1]

    st = lax.while_loop(cond, body, (i * nd - 1, m[0], m[1], l[0], l[1], a[0], a[1], f_first[0], f_first[1]))
    o_ref[...] = st[5] * (1.0 / st[3]) + st[6] * (1.0 / st[4])


def _fox_prompt(proj, fsum, fsum_t, n_seq, seq, tq, tk):
    t = proj.shape[0]
    nq = seq // tq
    cq, ck, cv = C_FOX // LANES, (C_FOX + FOX_W) // LANES, (C_FOX + 2 * FOX_W) // LANES
    return pl.pallas_call(
        functools.partial(_fox_prompt_kernel, tq=tq, tk=tk),
        out_shape=jax.ShapeDtypeStruct((t, FOX_W), F32),
        grid=(n_seq, FOX_HEADS // 2, nq),
        in_specs=[pl.BlockSpec((tq, LANES), lambda b, p, i: (b * nq + i, cq + p)),
                  pl.BlockSpec((seq, LANES), lambda b, p, i: (b, ck + p)),
                  pl.BlockSpec((seq, LANES), lambda b, p, i: (b, cv + p)),
                  pl.BlockSpec((tq, LANES), lambda b, p, i: (b * nq + i, 0)),
                  pl.BlockSpec((1, ROWPAD, seq), lambda b, p, i: (b, 0, 0))],
        out_specs=pl.BlockSpec((tq, LANES), lambda b, p, i: (b * nq + i, p)),
        scratch_shapes=[pltpu.SMEM((2,), F32)],
        compiler_params=_cparams(("arbitrary", "arbitrary", "arbitrary")),
        name="fox_prompt",
    )(proj, proj, proj, fsum, fsum_t)


def _block_diag_q(q, heads, dh):
    shape = (heads * ROWPAD, heads * dh)
    qt = jnp.concatenate([q] * heads, axis=0) * (dh ** -0.5)
    own = (_iota(shape, 0) // ROWPAD) == (_iota(shape, 1) // dh)
    return jnp.where(own, qt, 0.0).astype(BF16)


def _diag_rows(acc, heads, dh):
    own = (_iota(acc.shape, 0) // ROWPAD) == (_iota(acc.shape, 1) // dh)
    m = jnp.where(own, acc, 0.0)
    out = m[0:ROWPAD]
    for h in range(1, heads):
        out = out + m[h * ROWPAD:(h + 1) * ROWPAD]
    return out


def _lane_suffix(x):
    n = x.shape[-1]
    ax = len(x.shape) - 1
    lane = _iota(x.shape, ax)
    acc = x
    d = 1
    while d < n:
        shifted = pltpu.roll(acc, n - d, ax)
        acc = acc + jnp.where(lane < n - d, shifted, 0.0)
        d *= 2
    return acc - x, acc


def _expand_heads(x):
    return jnp.concatenate([jnp.broadcast_to(x[h:h + 1], (ROWPAD, x.shape[1])) for h in range(x.shape[0])],
                           axis=0)


def _page_bf16(page):
    return page.reshape(page.shape[0] * page.shape[1], page.shape[2]).astype(BF16)


def _sb_page(qbd, kpage, vpage, carry, visible):
    vf = _page_bf16(vpage)
    s = _dot(qbd, _page_bf16(kpage))
    sp = _softplus(s)
    lf = -sp
    if visible is not None:
        lf = jnp.where(visible, lf, 0.0)
    later, incl = _lane_suffix(lf)
    a = jnp.exp(s - sp + carry + later)
    if visible is not None:
        a = jnp.where(visible, a, 0.0)
    return _dot_nt(a.astype(BF16), vf), carry + incl[:, 0:1]


def _sb_decode_kernel(pt_ref, q_ref, kn_ref, vn_ref, k_hbm, v_hbm, o_ref, kbuf, vbuf, sem, *, layer, n_pages):
    b = pl.program_id(0)
    heads, dh = SB_HEADS, SB_DH
    rows = heads * ROWPAD
    qbd = _block_diag_q(q_ref[...], heads, dh)

    def copies(p, slot):
        page = pt_ref[b, p]
        return (pltpu.make_async_copy(k_hbm.at[layer, page], kbuf.at[slot], sem.at[0, slot]),
                pltpu.make_async_copy(v_hbm.at[layer, page], vbuf.at[slot], sem.at[1, slot]))

    def start(p, slot):
        for cp in copies(p, slot):
            cp.start()

    def wait(p, slot):
        for cp in copies(p, slot):
            cp.wait()

    start(n_pages - 1, 0)
    visible = _iota((rows, PAGE), 1) < (_iota((rows, PAGE), 0) % ROWPAD)
    acc, carry = _sb_page(qbd, kn_ref[0], vn_ref[0], jnp.zeros((rows, 1), F32), visible)

    def cond(st):
        p, cr, _ = st
        return jnp.logical_and(p >= 0, jnp.max(cr) > UNDERFLOW)

    def body(st):
        p, cr, ac = st
        slot = (n_pages - 1 - p) % 2
        wait(p, slot)

        @pl.when(p > 0)
        def _():
            start(p - 1, 1 - slot)

        d, cr = _sb_page(qbd, kbuf[slot], vbuf[slot], cr, None)
        return p - 1, cr, ac + d

    p_exit, carry, acc = lax.while_loop(cond, body, (n_pages - 1, carry, acc))

    @pl.when(p_exit >= 0)
    def _():
        wait(p_exit, (n_pages - 1 - p_exit) % 2)

    o_ref[...] = _diag_rows(acc, heads, dh)


def _sb_decode(page_table, proj, kn, vn, cache_k, cache_v, layer):
    n_seq, n_pages = page_table.shape
    heads, dh = SB_HEADS, SB_DH
    page_shape = (heads, dh, PAGE)
    new_spec = pl.BlockSpec((1,) + page_shape, lambda b, pt: (b, 0, 0, 0))
    grid_spec = pltpu.PrefetchScalarGridSpec(
        num_scalar_prefetch=1,
        grid=(n_seq,),
        in_specs=[pl.BlockSpec((ROWPAD, SB_W), lambda b, pt: (b, C_SB // SB_W)),
                  new_spec, new_spec,
                  pl.BlockSpec(memory_space=pl.ANY), pl.BlockSpec(memory_space=pl.ANY)],
        out_specs=pl.BlockSpec((ROWPAD, SB_W), lambda b, pt: (b, 0)),
        scratch_shapes=[pltpu.VMEM((2,) + page_shape, F32), pltpu.VMEM((2,) + page_shape, F32),
                        pltpu.SemaphoreType.DMA((2, 2))],
    )
    return pl.pallas_call(
        functools.partial(_sb_decode_kernel, layer=layer, n_pages=n_pages),
        out_shape=jax.ShapeDtypeStruct((n_seq * ROWPAD, SB_W), F32),
        grid_spec=grid_spec,
        compiler_params=_cparams(("arbitrary",)),
        name="sb_decode",
    )(page_table, proj, kn, vn, cache_k, cache_v)


def _page_suffix_kernel(x_ref, o_ref):
    tp, heads, n = x_ref.shape
    x = x_ref[...].reshape(tp * heads, n)
    ustrict = jnp.where(_iota((n, n), 0) > _iota((n, n), 1), 1.0, 0.0).astype(BF16)
    later = _dot_exact_rhs(x, ustrict)
    o_ref[:, 0:heads, :] = later.reshape(tp, heads, n)
    o_ref[:, heads:2 * heads, :] = (later + x).reshape(tp, heads, n)


def _page_suffix(lf_pages):
    depth, n_pool, heads, n = lf_pages.shape
    pages = depth * n_pool
    tp = _tile(pages, 256)
    out = pl.pallas_call(
        _page_suffix_kernel,
        out_shape=jax.ShapeDtypeStruct((pages, 2 * heads, n), F32),
        grid=(pages // tp,),
        in_specs=[pl.BlockSpec((tp, heads, n), lambda i: (i, 0, 0))],
        out_specs=pl.BlockSpec((tp, 2 * heads, n), lambda i: (i, 0, 0)),
        compiler_params=_cparams(("parallel",)),
        name="page_suffix",
    )(lf_pages.reshape(pages, heads, n))
    return out.reshape(depth, n_pool, 2 * heads, n)


def _fox_decode_kernel(pt_ref, q_ref, kn_ref, vn_ref, pc_ref, pn_ref, k_hbm, v_hbm, sf_hbm, o_ref,
                       kbuf, vbuf, sfbuf, sem, *, layer, n_pages, group, n_seq):
    b = pl.program_id(0)
    heads, dh = FOX_HEADS, FOX_DH
    rows = heads * ROWPAD
    n_groups = n_pages // group
    n_slots = kbuf.shape[0]
    ahead = min(n_slots - 1, n_groups)
    qbd = _block_diag_q(q_ref[...], heads, dh)
    pcol = pc_ref[0]

    def copies(seq, gi, slot):
        out = []
        for g in range(group):
            page = pt_ref[seq, n_pages - 1 - gi * group - g]
            for a, (hbm, buf) in enumerate(((k_hbm, kbuf), (v_hbm, vbuf), (sf_hbm, sfbuf))):
                out.append(pltpu.make_async_copy(hbm.at[layer, page], buf.at[slot, g], sem.at[a, slot]))
        return out

    def start(seq, gi, slot):
        for cp in copies(seq, gi, slot):
            cp.start()

    def wait(seq, gi, slot):
        for cp in copies(seq, gi, slot):
            cp.wait()

    @pl.when(b == 0)
    def _():
        for gi in range(ahead):
            start(0, gi, gi)

    visible = _iota((rows, PAGE), 1) <= (_iota((rows, PAGE), 0) % ROWPAD)
    s = _dot(qbd, _page_bf16(kn_ref[0])) + (pcol - _expand_heads(pn_ref[0]))
    s = jnp.where(visible, s, -jnp.inf)
    m = jnp.max(s, axis=1, keepdims=True)
    pr = jnp.exp(s - m)
    l = jnp.sum(pr, axis=1, keepdims=True)
    acc = _dot_nt(pr.astype(BF16), _page_bf16(vn_ref[0]))

    def body(gi, st):
        m_old, l_old, acc_old, run = st
        slot = gi % n_slots
        wait(b, gi, slot)

        @pl.when(gi + ahead < n_groups)
        def _():
            start(b, gi + ahead, (gi + ahead) % n_slots)

        parts = []
        for g in range(group):
            bias = pcol + _expand_heads(run + sfbuf[slot, g, 0:heads, :])
            parts.append(_dot(qbd, _page_bf16(kbuf[slot, g])) + bias)
            run = run + sfbuf[slot, g, heads:2 * heads, 0:1]
        s = jnp.concatenate(parts, axis=1)
        m_new = jnp.maximum(m_old, jnp.max(s, axis=1, keepdims=True))
        alpha = jnp.exp(m_old - m_new)
        pr = jnp.exp(s - m_new)
        l_new = l_old * alpha + jnp.sum(pr, axis=1, keepdims=True)
        pv = _dot_nt(pr[:, 0:PAGE].astype(BF16), _page_bf16(vbuf[slot, 0]))
        for g in range(1, group):
            pv = pv + _dot_nt(pr[:, g * PAGE:(g + 1) * PAGE].astype(BF16), _page_bf16(vbuf[slot, g]))
        return m_new, l_new, acc_old * alpha + pv, run

    m, l, acc, _ = lax.fori_loop(0, n_groups, body, (m, l, acc, jnp.zeros((heads, PAGE), F32)))

    @pl.when(b + 1 < n_seq)
    def _():
        for gi in range(ahead):
            start(b + 1, gi, gi)

    o_ref[...] = _diag_rows(acc * (1.0 / l), heads, dh)


FOX_PAGE_SLOTS = 3


def _fox_decode(page_table, proj, kn, vn, pcol, pnew, cache_k, cache_v, lf_suffix, layer):
    n_seq, n_pages = page_table.shape
    heads, dh = FOX_HEADS, FOX_DH
    rows = heads * ROWPAD
    page_shape = (heads, dh, PAGE)
    group = next(g for g in (16, 8, 4, 2, 1) if n_pages % g == 0)
    new_spec = pl.BlockSpec((1,) + page_shape, lambda b, pt: (b, 0, 0, 0))
    any_spec = pl.BlockSpec(memory_space=pl.ANY)
    bufs = (FOX_PAGE_SLOTS, group)
    grid_spec = pltpu.PrefetchScalarGridSpec(
        num_scalar_prefetch=1,
        grid=(n_seq,),
        in_specs=[pl.BlockSpec((ROWPAD, FOX_W), lambda b, pt: (b, C_FOX // FOX_W)),
                  new_spec, new_spec,
                  pl.BlockSpec((1, rows, 1), lambda b, pt: (b, 0, 0)),
                  pl.BlockSpec((1, heads, PAGE), lambda b, pt: (b, 0, 0)),
                  any_spec, any_spec, any_spec],
        out_specs=pl.BlockSpec((ROWPAD, FOX_W), lambda b, pt: (b, 0)),
        scratch_shapes=[pltpu.VMEM(bufs + page_shape, F32), pltpu.VMEM(bufs + page_shape, F32),
                        pltpu.VMEM(bufs + (2 * heads, PAGE), F32),
                        pltpu.SemaphoreType.DMA((3, FOX_PAGE_SLOTS))],
    )
    return pl.pallas_call(
        functools.partial(_fox_decode_kernel, layer=layer, n_pages=n_pages, group=group, n_seq=n_seq),
        out_shape=jax.ShapeDtypeStruct((n_seq * ROWPAD, FOX_W), F32),
        grid_spec=grid_spec,
        compiler_params=_cparams(("arbitrary",)),
        name="fox_decode",
    )(page_table, proj, kn, vn, pcol, pnew, cache_k, cache_v, lf_suffix)


def _merge_kernel(x_ref, go_ref, sb_ref, fx_ref, g0_ref, g1_ref, g2_ref, gt_ref,
                  wg_ref, ws_ref, wf_ref, wo_ref, o_ref):
    rows = x_ref.shape[0]
    a = _dot(go_ref[...].astype(BF16), wg_ref[...])
    b = _dot(sb_ref[...].astype(BF16), ws_ref[...])
    c = _dot(fx_ref[...].astype(BF16), wf_ref[...])
    merged = _sigmoid(g0_ref[...]) * a + _sigmoid(g1_ref[...]) * b + _sigmoid(g2_ref[...]) * c
    o_ref[...] = x_ref[...] + _mod_rows(gt_ref, rows) * _dot(merged.astype(BF16), wo_ref[...])


def _merge(x, go, sbo, foxo, proj, gt, wg, ws, wf, wo, rows_per_seq, tm):
    t, d = x.shape
    cg = C_GATE // d
    row = lambda w: pl.BlockSpec((tm, w), lambda i: (i, 0))
    full = lambda a: pl.BlockSpec(a.shape, lambda i: (0, 0))
    return pl.pallas_call(
        _merge_kernel,
        out_shape=jax.ShapeDtypeStruct((t, d), F32),
        grid=(t // tm,),
        in_specs=[row(d), row(GDN_V), row(SB_W), row(FOX_W),
                  pl.BlockSpec((tm, d), lambda i: (i, cg)),
                  pl.BlockSpec((tm, d), lambda i: (i, cg + 1)),
                  pl.BlockSpec((tm, d), lambda i: (i, cg + 2)),
                  _mod_spec(tm, rows_per_seq, d, False),
                  full(wg), full(ws), full(wf), full(wo)],
        out_specs=row(d),
        compiler_params=_cparams(("parallel",)),
        name="merge_out",
    )(x, go, sbo, foxo, proj, proj, proj, gt, wg, ws, wf, wo)


def _ffn_down_kernel(up_ref, prev_ref, x_ref, gt_ref, cw_ref, cb_ref, wd_ref, o_ref, xs, *, cwid):
    tm = up_ref.shape[0]
    d_ff = wd_ref.shape[0]
    halo = FFN_CONV - 1
    xs[0:ROWPAD, :] = prev_ref[0]
    xs[ROWPAD:ROWPAD + tm, :] = up_ref[...]

    def conv(c0):
        cs = slice(c0, c0 + cwid)
        y = cb_ref[:, cs] + xs[ROWPAD - halo:ROWPAD - halo + tm, cs] * cw_ref[0:1, cs]
        for i in range(1, FFN_CONV):
            y = y + xs[ROWPAD - halo + i:ROWPAD - halo + i + tm, cs] * cw_ref[i:i + 1, cs]
        return y

    acc = jnp.zeros((tm, o_ref.shape[1]), F32)
    for c0 in range(0, d_ff, cwid):
        hid = _silu(conv(c0)) * conv(d_ff + c0)
        acc = acc + _dot(hid.astype(BF16), wd_ref[c0:c0 + cwid, :])
    o_ref[...] = x_ref[...] + _mod_rows(gt_ref, tm) * acc


def _ffn_down(up, prev, x, gt, conv_w, conv_b, wd, rows_per_seq, tm):
    t, d = x.shape
    n2 = up.shape[1]
    d_ff = n2 // 2
    cwid = 256 if d_ff % 256 == 0 else LANES
    if rows_per_seq == ROWPAD:
        gt_spec = pl.BlockSpec((tm, d), lambda i: (i, 0))
    else:
        gt_spec = pl.BlockSpec((ROWPAD, d), lambda i: ((i * tm) // rows_per_seq, 0))
    return pl.pallas_call(
        functools.partial(_ffn_down_kernel, cwid=cwid),
        out_shape=jax.ShapeDtypeStruct((t, d), F32),
        grid=(t // tm,),
        in_specs=[pl.BlockSpec((tm, n2), lambda i: (i, 0)),
                  pl.BlockSpec((1, ROWPAD, n2), lambda i: (i, 0, 0)),
                  pl.BlockSpec((tm, d), lambda i: (i, 0)),
                  gt_spec,
                  pl.BlockSpec((FFN_CONV, n2), lambda i: (0, 0)),
                  pl.BlockSpec((1, n2), lambda i: (0, 0)),
                  pl.BlockSpec((d_ff, d), lambda i: (0, 0))],
        out_specs=pl.BlockSpec((tm, d), lambda i: (i, 0)),
        scratch_shapes=[pltpu.VMEM((ROWPAD + tm, n2), F32)],
        compiler_params=_cparams(("parallel",)),
        name="ffn_down",
    )(up, prev, x, gt, conv_w, conv_b.reshape(1, n2), wd)


def _final_norm_kernel(x_ref, g_ref, o_ref):
    x = x_ref[...]
    o_ref[...] = x * lax.rsqrt(jnp.mean(x * x, axis=-1, keepdims=True) + EPS) * g_ref[...]


def _final_norm(x, g, tm):
    t, d = x.shape
    return pl.pallas_call(
        _final_norm_kernel,
        out_shape=jax.ShapeDtypeStruct((t, d), F32),
        grid=(t // tm,),
        in_specs=[pl.BlockSpec((tm, d), lambda i: (i, 0)), pl.BlockSpec((1, d), lambda i: (0, 0))],
        out_specs=pl.BlockSpec((tm, d), lambda i: (i, 0)),
        compiler_params=_cparams(("parallel",)),
        name="final_norm",
    )(x, g.reshape(1, d))


def _tile(n, pref):
    t = min(pref, n)
    while n % t:
        t //= 2
    return t


def _prep_w_in(w_in, d):
    o_a = GDN_CH
    o_b = o_a + GDN_HEADS
    o_z = o_b + GDN_HEADS
    o_sb = o_z + GDN_V
    o_fox = o_sb + 3 * SB_W
    o_f = o_fox + 3 * FOX_W
    o_gate = o_f + FOX_HEADS
    small = jnp.concatenate([w_in[:, o_f:o_gate], w_in[:, o_a:o_b], w_in[:, o_b:o_z]], axis=1)
    small = jnp.pad(small, ((0, 0), (0, LANES - small.shape[1])))
    w = jnp.concatenate([w_in[:, :o_a], w_in[:, o_z:o_sb], w_in[:, o_sb:o_fox], w_in[:, o_fox:o_f],
                         w_in[:, o_gate:], small], axis=1)
    return w.astype(BF16)


def _rep(v):
    return jnp.repeat(v, ROWPAD, axis=0)


def _layer_group(x, mod, lw, rows_per_seq, n_valid, conv0, s0, ffn_prev_fn, attn_fn):
    t, d = x.shape
    n_seq = t // rows_per_seq
    sh1, sc1, gt1, sh2, sc2, gt2 = [_rep(m) for m in jnp.split(mod, 6, axis=-1)]
    tm = _tile(t if rows_per_seq == ROWPAD else rows_per_seq, 1024)
    n_proj = lw["w_in"].shape[1]
    proj = _norm_matmul(x, lw["norm1_g"], sc1, sh1, lw["w_in"], rows_per_seq, tm,
                        n_proj // 5, "proj_in")
    chunk = GDN_CHUNK if rows_per_seq % GDN_CHUNK == 0 else rows_per_seq
    go, s_new = _gdn(proj, conv0, s0, lw["gdn_conv_w"], lw["gdn_a_log"], lw["gdn_dt_bias"],
                     lw["gdn_norm_g"], n_seq, rows_per_seq, chunk, min(n_valid, chunk))
    logf, fsum, fsum_t = _fox_pre(proj, lw["fox_f_bias"], n_seq, rows_per_seq, _tile(rows_per_seq, 256))
    sbo, foxo = attn_fn(proj, fsum, fsum_t)
    x1 = _merge(x, go, sbo, foxo, proj, gt1, lw["w_br_gdn"], lw["w_br_sb"], lw["w_br_fox"], lw["w_out"],
                rows_per_seq, _tile(t, 256))
    n_up = lw["w_up"].shape[1]
    up = _norm_matmul(x1, lw["norm2_g"], sc2, sh2, lw["w_up"], rows_per_seq, tm, n_up // 4, "ffn_up")
    tm_d = ROWPAD if rows_per_seq == ROWPAD else _tile(rows_per_seq, 256)
    x2 = _ffn_down(up, ffn_prev_fn(up, tm_d), x1, gt2, lw["ffn_conv_w"], lw["ffn_conv_b"], lw["w_down"],
                   rows_per_seq, tm_d)
    return x2, proj, up, logf, s_new


def kernel(x_prompt, x_sample, cache_sb_k, cache_sb_v, cache_fox_k, cache_fox_v, cache_fox_logf, state_gdn, state_gdn_conv, state_ffn_conv, page_table, c_prompt, c_sample, norm1_g, norm2_g, final_g, w_ada, b_ada, w_in, gdn_conv_w, gdn_a_log, gdn_dt_bias, gdn_norm_g, fox_f_bias, w_br_gdn, w_br_sb, w_br_fox, w_out, w_up, ffn_conv_w, ffn_conv_b, w_down):
    bp, seq, d = x_prompt.shape
    bs, dseq, _ = x_sample.shape
    depth = w_in.shape[0]
    n2 = w_up.shape[2]
    assert dseq <= ROWPAD and seq % GDN_CHUNK == 0 and dseq >= GDN_CONV - 1

    xp = x_prompt.reshape(bp * seq, d)
    xs = jnp.pad(x_sample, ((0, 0), (0, ROWPAD - dseq), (0, 0))).reshape(bs * ROWPAD, d)
    n_c = bp + bs
    c_all = jnp.pad(jnp.concatenate([c_prompt, c_sample], axis=0), ((0, (-n_c) % ROWPAD), (0, 0)))
    sbk_pages, sbv_pages, fxk_pages, fxv_pages = [
        jnp.transpose(a, (0, 1, 3, 4, 2)) for a in (cache_sb_k, cache_sb_v, cache_fox_k, cache_fox_v)]
    lf_suffix = _page_suffix(jnp.transpose(cache_fox_logf, (0, 1, 3, 2)))

    def new_page(a, heads, dh):
        a = a.reshape(bs, ROWPAD, heads, dh).transpose(0, 2, 3, 1)
        return jnp.pad(a, ((0, 0), (0, 0), (0, 0), (0, PAGE - ROWPAD)))

    outs_p, outs_s = [], []
    for l in range(depth):
        lw = {"norm1_g": norm1_g[l], "norm2_g": norm2_g[l], "w_in": _prep_w_in(w_in[l], d),
              "gdn_conv_w": gdn_conv_w[l], "gdn_a_log": gdn_a_log[l], "gdn_dt_bias": gdn_dt_bias[l],
              "gdn_norm_g": gdn_norm_g[l], "fox_f_bias": fox_f_bias[l],
              "w_br_gdn": w_br_gdn[l].astype(BF16), "w_br_sb": w_br_sb[l].astype(BF16),
              "w_br_fox": w_br_fox[l].astype(BF16), "w_out": w_out[l].astype(BF16),
              "w_up": w_up[l].astype(BF16), "ffn_conv_w": ffn_conv_w[l], "ffn_conv_b": ffn_conv_b[l],
              "w_down": w_down[l].astype(BF16)}
        mod = _ada(c_all, w_ada[l], b_ada[l])

        def prompt_attn(proj, fsum, fsum_t):
            tq = _tile(seq, 256)
            tk = _tile(tq, 128)
            tf = _tile(seq, 512)
            return (_sb_prompt(proj, bp, seq, tq, tk), _fox_prompt(proj, fsum, fsum_t, bp, seq, tf, tf))

        def prompt_prev(up, tm_d):
            nt = up.shape[0] // tm_d
            tails = up.reshape(nt, tm_d, n2)[:, tm_d - ROWPAD:, :]
            prev = jnp.concatenate([jnp.zeros((1, ROWPAD, n2), F32), tails[:-1]], axis=0)
            first = (jnp.arange(nt) * tm_d) % seq == 0
            return jnp.where(first[:, None, None], 0.0, prev)

        xp, proj, up, logf, s_new = _layer_group(
            xp, mod[:bp], lw, seq, seq,
            jnp.zeros((bp, ROWPAD, GDN_CH), F32), jnp.zeros((bp, GDN_HEADS, GDN_DK, GDN_DV), F32),
            prompt_prev, prompt_attn)
        p3 = proj.reshape(bp, seq, -1)
        outs_p.append((
            p3[:, :, C_SB + SB_W:C_SB + 2 * SB_W].reshape(bp, seq, SB_HEADS, SB_DH),
            p3[:, :, C_SB + 2 * SB_W:C_SB + 3 * SB_W].reshape(bp, seq, SB_HEADS, SB_DH),
            p3[:, :, C_FOX + FOX_W:C_FOX + 2 * FOX_W].reshape(bp, seq, FOX_HEADS, FOX_DH),
            p3[:, :, C_FOX + 2 * FOX_W:C_FOX + 3 * FOX_W].reshape(bp, seq, FOX_HEADS, FOX_DH),
            logf.reshape(bp, seq, LANES)[:, :, SM_F:SM_F + FOX_HEADS],
            s_new,
            p3[:, seq - (GDN_CONV - 1):, C_QKV:C_QKV + GDN_CH],
            up.reshape(bp, seq, n2)[:, seq - (FFN_CONV - 1):, :]))

        def sample_attn(proj, fsum, fsum_t, l=l):
            def cols(c0, w):
                return proj[:, c0:c0 + w]
            sbo = _sb_decode(page_table, proj,
                             new_page(cols(C_SB + SB_W, SB_W), SB_HEADS, SB_DH),
                             new_page(cols(C_SB + 2 * SB_W, SB_W), SB_HEADS, SB_DH),
                             sbk_pages, sbv_pages, l)
            pcol = fsum_t.reshape(bs, FOX_HEADS * ROWPAD, 1)
            pnew = jnp.pad(fsum_t, ((0, 0), (0, 0), (0, PAGE - ROWPAD)))
            foxo = _fox_decode(page_table, proj,
                               new_page(cols(C_FOX + FOX_W, FOX_W), FOX_HEADS, FOX_DH),
                               new_page(cols(C_FOX + 2 * FOX_W, FOX_W), FOX_HEADS, FOX_DH),
                               pcol, pnew, fxk_pages, fxv_pages, lf_suffix, l)
            return sbo, foxo

        def sample_prev(up, tm_d, l=l):
            return jnp.pad(state_ffn_conv[l], ((0, 0), (ROWPAD - (FFN_CONV - 1), 0), (0, 0)))

        conv0 = jnp.pad(state_gdn_conv[l], ((0, 0), (ROWPAD - (GDN_CONV - 1), 0), (0, 0)))
        xs, proj, up, logf, s_new = _layer_group(
            xs, mod[bp:bp + bs], lw, ROWPAD, dseq, conv0, state_gdn[l], sample_prev, sample_attn)
        p3 = proj.reshape(bs, ROWPAD, -1)
        outs_s.append((
            p3[:, :dseq, C_SB + SB_W:C_SB + 2 * SB_W].reshape(bs, dseq, SB_HEADS, SB_DH),
            p3[:, :dseq, C_SB + 2 * SB_W:C_SB + 3 * SB_W].reshape(bs, dseq, SB_HEADS, SB_DH),
            p3[:, :dseq, C_FOX + FOX_W:C_FOX + 2 * FOX_W].reshape(bs, dseq, FOX_HEADS, FOX_DH),
            p3[:, :dseq, C_FOX + 2 * FOX_W:C_FOX + 3 * FOX_W].reshape(bs, dseq, FOX_HEADS, FOX_DH),
            logf.reshape(bs, ROWPAD, LANES)[:, :dseq, SM_F:SM_F + FOX_HEADS],
            s_new,
            p3[:, dseq - (GDN_CONV - 1):dseq, C_QKV:C_QKV + GDN_CH],
            up.reshape(bs, ROWPAD, n2)[:, dseq - (FFN_CONV - 1):dseq, :]))

    y_p = _final_norm(xp, final_g, _tile(xp.shape[0], 512)).reshape(bp, seq, d)
    y_s = _final_norm(xs, final_g, _tile(xs.shape[0], 512)).reshape(bs, ROWPAD, d)[:, :dseq]
    P = [jnp.stack([st[i] for st in outs_p]) for i in range(8)]
    S = [jnp.stack([st[i] for st in outs_s]) for i in range(8)]
    return (y_p, y_s, P[0], S[0], P[1], S[1], P[2], S[2], P[3], S[3],
            P[4], S[4], P[5], S[5], P[6], S[6], P[7], S[7])
```
